```python
import jax, jax.numpy as jnp
from jax import lax
import numpy as np

D_MODEL = 4096
BATCH = 4
SEQ = 4096
DEPTH = 2

CTX_LEN = 256
GRID_W = 64
N_MIXERS = 2
D_RNN = D_MODEL
RNN_BLOCKS = 16
RNN_BW = D_RNN // RNN_BLOCKS
CONV_W = 4
CONV_PAD_LO = 2
LRU_C = 8.0
LRU_A_MIN = 0.9
LRU_A_MAX = 0.999
HEAD_DIM = 128
N_HEADS = D_MODEL // HEAD_DIM
N_KV_HEADS = 8
GQA_GROUPS = N_HEADS // N_KV_HEADS
ATTN_WIDTH = N_HEADS * HEAD_DIM
KV_WIDTH = N_KV_HEADS * HEAD_DIM
WINDOW = 128
ATTN_BLOCK = 128
ROPE_BASE = 10000.0
NORM_EPS = 1e-6
NEG_INF = -1e30

kernel_name = 'hybrid_rglru_swa_dit_prefix'


def rms_norm(x, w):
    xf = x.astype(jnp.float32)
    xf = xf * lax.rsqrt(jnp.mean(xf * xf, axis=-1, keepdims=True) + NORM_EPS)
    return (xf * w.astype(jnp.float32)).astype(x.dtype)


def adaln_input(x, norm_w, shift, scale):
    return rms_norm(x, norm_w) * (1.0 + scale) + shift


def depthwise_conv(u, w, b):
    out = lax.conv_general_dilated(
        u, w[:, None, :], window_strides=(1,),
        padding=[(CONV_PAD_LO, CONV_W - 1 - CONV_PAD_LO)],
        dimension_numbers=('NWC', 'WIO', 'NWC'),
        feature_group_count=u.shape[-1])
    return out + b


def rglru_coeffs(u, w_r, b_r, w_i, b_i, lam):
    uf = u.astype(jnp.float32)
    ub = uf.reshape(uf.shape[:-1] + (RNN_BLOCKS, RNN_BW))
    r = jax.nn.sigmoid(jnp.einsum('blhi,hij->blhj', ub, w_r.astype(jnp.float32)).reshape(uf.shape)
                       + b_r.astype(jnp.float32))
    i = jax.nn.sigmoid(jnp.einsum('blhi,hij->blhj', ub, w_i.astype(jnp.float32)).reshape(uf.shape)
                       + b_i.astype(jnp.float32))
    log_a = -LRU_C * r * jax.nn.softplus(-lam.astype(jnp.float32))
    a = jnp.exp(log_a)
    b = jnp.sqrt(-jnp.expm1(2.0 * log_a)) * (i * uf)
    return a, b


def _lin_combine(e1, e2):
    a1, b1 = e1
    a2, b2 = e2
    return a1 * a2, a2 * b1 + b2


def linear_scan(a, b, reverse):
    return lax.associative_scan(_lin_combine, (a, b), axis=1, reverse=reverse)[1]


def rglru_branch(h, hc, w_in, conv_w, conv_b, w_r, b_r, w_i, b_i, lam, w_out, with_ctx_out):
    u_lat, g_lat = jnp.split(h @ w_in, 2, axis=-1)
    if with_ctx_out:
        u_ctx, g_ctx = jnp.split(hc @ w_in, 2, axis=-1)
    else:
        u_ctx = hc @ w_in[:, :D_RNN]
    u_lat = depthwise_conv(u_lat, conv_w, conv_b)
    u_ctx = depthwise_conv(u_ctx, conv_w, conv_b)
    lat_states, ctx_states = [], []
    for d, reverse in enumerate((False, True)):
        a_c, b_c = rglru_coeffs(u_ctx, w_r[d], b_r[d], w_i[d], b_i[d], lam[d])
        s_ctx = linear_scan(a_c, b_c, reverse)
        h0 = s_ctx[:, 0] if reverse else s_ctx[:, -1]
        a_l, b_l = rglru_coeffs(u_lat, w_r[d], b_r[d], w_i[d], b_i[d], lam[d])
        start = -1 if reverse else 0
        b_l = b_l.at[:, start].add(a_l[:, start] * h0)
        lat_states.append(linear_scan(a_l, b_l, reverse))
        ctx_states.append(s_ctx)
    y_lat = ((lat_states[0] + lat_states[1]).astype(h.dtype) * jax.nn.silu(g_lat)) @ w_out
    y_ctx = None
    if with_ctx_out:
        y_ctx = ((ctx_states[0] + ctx_states[1]).astype(hc.dtype) * jax.nn.silu(g_ctx)) @ w_out
    return y_lat, y_ctx


def head_rms_norm(t, w):
    return rms_norm(t, w)


def grid_positions(S):
    rows = S // GRID_W
    row = jnp.repeat(jnp.arange(rows, dtype=jnp.int32), GRID_W)
    col = jnp.tile(jnp.arange(GRID_W, dtype=jnp.int32), rows)
    return row, col


def rope_1d(xf, pos, dim):
    half = dim // 2
    inv_freq = ROPE_BASE ** (-jnp.arange(half, dtype=jnp.float32) * (2.0 / dim))
    ang = pos.astype(jnp.float32)[:, None] * inv_freq[None, :]
    cos = jnp.cos(ang)[None, :, None, :]
    sin = jnp.sin(ang)[None, :, None, :]
    x1, x2 = xf[..., :half], xf[..., half:]
    return jnp.concatenate([x1 * cos - x2 * sin, x2 * cos + x1 * sin], axis=-1)


def rope_2d(t, row, col):
    tf = t.astype(jnp.float32)
    hd = t.shape[-1] // 2
    out = jnp.concatenate([rope_1d(tf[..., :hd], row, hd), rope_1d(tf[..., hd:], col, hd)], axis=-1)
    return out.astype(t.dtype)


def sink_column(sink, like):
    s = sink.astype(jnp.float32).reshape(N_KV_HEADS, GQA_GROUPS)[:, :, None, None]
    return jnp.broadcast_to(s, like.shape[:-1] + (1,))


def banded_window_attention(q, k, v, kc, vc, sink):
    B, S = q.shape[:2]
    C = kc.shape[1]
    NB = S // ATTN_BLOCK
    scale = HEAD_DIM ** -0.5
    qb = q.reshape(B, NB, ATTN_BLOCK, N_KV_HEADS, GQA_GROUPS, HEAD_DIM)

    def band(t):
        tb = t.reshape(B, NB, ATTN_BLOCK, N_KV_HEADS, HEAD_DIM)
        tp = jnp.pad(tb, ((0, 0), (1, 1), (0, 0), (0, 0), (0, 0)))
        return jnp.concatenate([tp[:, :-2], tp[:, 1:-1], tp[:, 2:]], axis=2)

    kw, vw = band(k), band(v)
    s_win = jnp.einsum('bnqkgd,bnskd->bnkgqs', qb, kw, preferred_element_type=jnp.float32) * scale
    s_ctx = jnp.einsum('bnqkgd,bckd->bnkgqc', qb, kc, preferred_element_type=jnp.float32) * scale
    qi = jnp.arange(ATTN_BLOCK)[:, None]
    ki = jnp.arange(3 * ATTN_BLOCK)[None, :]
    rel = ki - ATTN_BLOCK - qi
    key_pos = (jnp.arange(NB)[:, None, None] - 1) * ATTN_BLOCK + ki[None]
    mask = (jnp.abs(rel) <= WINDOW)[None] & (key_pos >= 0) & (key_pos < S)
    s_win = jnp.where(mask[None, :, None, None], s_win, NEG_INF)
    logits = jnp.concatenate([sink_column(sink, s_ctx), s_ctx, s_win], axis=-1)
    p = jax.nn.softmax(logits, axis=-1).astype(v.dtype)
    o = (jnp.einsum('bnkgqc,bckd->bnqkgd', p[..., 1:1 + C], vc)
         + jnp.einsum('bnkgqs,bnskd->bnqkgd', p[..., 1 + C:], vw))
    return o.reshape(B, S, ATTN_WIDTH)


def context_attention(qc, kc, vc, sink):
    B, C = qc.shape[:2]
    scale = HEAD_DIM ** -0.5
    qg = qc.reshape(B, C, N_KV_HEADS, GQA_GROUPS, HEAD_DIM)
    s = jnp.einsum('bqkgd,bckd->bkgqc', qg, kc, preferred_element_type=jnp.float32) * scale
    p = jax.nn.softmax(jnp.concatenate([sink_column(sink, s), s], axis=-1), axis=-1).astype(vc.dtype)
    o = jnp.einsum('bkgqc,bckd->bqkgd', p[..., 1:], vc)
    return o.reshape(B, C, ATTN_WIDTH)


def window_attn_branch(h, hc, w_in, q_norm, k_norm, sink, w_out, with_ctx_out):
    B, S, _ = h.shape
    C = hc.shape[1]
    cuts = [ATTN_WIDTH, ATTN_WIDTH + KV_WIDTH, ATTN_WIDTH + 2 * KV_WIDTH]
    q, k, v, g = jnp.split(h @ w_in, cuts, axis=-1)
    q = head_rms_norm(q.reshape(B, S, N_HEADS, HEAD_DIM), q_norm)
    k = head_rms_norm(k.reshape(B, S, N_KV_HEADS, HEAD_DIM), k_norm)
    v = v.reshape(B, S, N_KV_HEADS, HEAD_DIM)
    row, col = grid_positions(S)
    q = rope_2d(q, row, col)
    k = rope_2d(k, row, col)
    if with_ctx_out:
        qc, kc, vc, gc = jnp.split(hc @ w_in, cuts, axis=-1)
    else:
        kc, vc = jnp.split(hc @ w_in[:, ATTN_WIDTH:ATTN_WIDTH + 2 * KV_WIDTH], 2, axis=-1)
    kc = head_rms_norm(kc.reshape(B, C, N_KV_HEADS, HEAD_DIM), k_norm)
    vc = vc.reshape(B, C, N_KV_HEADS, HEAD_DIM)
    o = banded_window_attention(q, k, v, kc, vc, sink)
    y_lat = (o * jax.nn.silu(g)) @ w_out
    y_ctx = None
    if with_ctx_out:
        qc = head_rms_norm(qc.reshape(B, C, N_HEADS, HEAD_DIM), q_norm)
        oc = context_attention(qc, kc, vc, sink)
        y_ctx = (oc * jax.nn.silu(gc)) @ w_out
    return y_lat, y_ctx


def setup_inputs(seed: int = 0) -> dict:
    key = jax.random.key(seed)
    ks = jax.random.split(key, 24)
    f32 = jnp.float32
    n_a = (DEPTH + 1) // 2
    n_b = DEPTH // 2
    nrm = lambda k, shape, s: jax.random.normal(k, shape, f32) * s
    u = jax.random.uniform(ks[13], (n_a, 2, D_RNN), f32, LRU_A_MIN, LRU_A_MAX)
    a0 = u ** (1.0 / LRU_C)
    return {
        'x': nrm(ks[0], (BATCH, SEQ, D_MODEL), 1.0),
        'c': nrm(ks[1], (BATCH, D_MODEL), 1.0),
        'ctx': nrm(ks[2], (BATCH, CTX_LEN, D_MODEL), 1.0),
        'c_ctx': nrm(ks[3], (D_MODEL,), 1.0),
        'w_mod': nrm(ks[4], (DEPTH, D_MODEL, 3 * D_MODEL), D_MODEL ** -0.5),
        'b_mod': nrm(ks[5], (DEPTH, 3 * D_MODEL), 0.01),
        'norm_w': 1.0 + nrm(ks[6], (DEPTH, D_MODEL), 0.02),
        'rg_w_in': nrm(ks[7], (n_a, D_MODEL, 2 * D_RNN), D_MODEL ** -0.5),
        'rg_conv_w': nrm(ks[8], (n_a, CONV_W, D_RNN), CONV_W ** -0.5),
        'rg_conv_b': nrm(ks[9], (n_a, D_RNN), 0.01),
        'rg_w_r': nrm(ks[10], (n_a, 2, RNN_BLOCKS, RNN_BW, RNN_BW), RNN_BW ** -0.5),
        'rg_b_r': nrm(ks[11], (n_a, 2, D_RNN), 0.01),
        'rg_w_i': nrm(ks[12], (n_a, 2, RNN_BLOCKS, RNN_BW, RNN_BW), RNN_BW ** -0.5),
        'rg_b_i': nrm(ks[14], (n_a, 2, D_RNN), 0.01),
        'rg_lam': jnp.log(a0) - jnp.log1p(-a0),
        'rg_w_out': nrm(ks[15], (n_a, D_RNN, D_MODEL), D_RNN ** -0.5),
        'at_w_in': nrm(ks[16], (n_b, D_MODEL, 2 * ATTN_WIDTH + 2 * KV_WIDTH), D_MODEL ** -0.5),
        'at_q_norm': 1.0 + nrm(ks[17], (n_b, HEAD_DIM), 0.02),
        'at_k_norm': 1.0 + nrm(ks[18], (n_b, HEAD_DIM), 0.02),
        'at_sink': nrm(ks[19], (n_b, N_HEADS), 0.5),
        'at_w_out': nrm(ks[20], (n_b, ATTN_WIDTH, D_MODEL), ATTN_WIDTH ** -0.5),
    }


def reference(x, c, ctx, c_ctx, w_mod, b_mod, norm_w,
              rg_w_in, rg_conv_w, rg_conv_b, rg_w_r, rg_b_r, rg_w_i, rg_b_i, rg_lam, rg_w_out,
              at_w_in, at_q_norm, at_k_norm, at_sink, at_w_out):
    for layer in range(DEPTH):
        with_ctx_out = layer < DEPTH - 1
        mod = jax.nn.silu(c) @ w_mod[layer] + b_mod[layer]
        shift, scale, gate = jnp.split(mod, 3, axis=-1)
        mod_c = jax.nn.silu(c_ctx) @ w_mod[layer] + b_mod[layer]
        shift_c, scale_c, gate_c = jnp.split(mod_c, 3, axis=-1)
        h = adaln_input(x, norm_w[layer], shift[:, None, :], scale[:, None, :])
        hc = adaln_input(ctx, norm_w[layer], shift_c, scale_c)
        j = layer // N_MIXERS
        if layer % N_MIXERS == 0:
            y, yc = rglru_branch(h, hc, rg_w_in[j], rg_conv_w[j], rg_conv_b[j], rg_w_r[j], rg_b_r[j],
                                 rg_w_i[j], rg_b_i[j], rg_lam[j], rg_w_out[j], with_ctx_out)
        else:
            y, yc = window_attn_branch(h, hc, at_w_in[j], at_q_norm[j], at_k_norm[j], at_sink[j],
                                       at_w_out[j], with_ctx_out)
        x = x + gate[:, None, :] * y
        if with_ctx_out:
            ctx = ctx + gate_c * yc
    return x
```

```python
import functools

import jax
import jax.numpy as jnp
from jax import lax
from jax.experimental import pallas as pl
from jax.experimental.pallas import tpu as pltpu

F32 = jnp.float32
BF16 = jnp.bfloat16

HEAD_DIM = 128
WINDOW = 128
ATTN_BLOCK = 128
GRID_W = 64
ROPE_BASE = 10000.0
NORM_EPS = 1e-6
NEG_INF = -1e30
LRU_C = 8.0
CONV_W = 4
CONV_PAD_LO = 2

SUBLANES = 8
LANES = 128
BF16_ROWS = 16
MOD_ROWS = 8
VMEM_LIMIT = 56 * 1024 * 1024


def _cparams(*sem):
    return pltpu.CompilerParams(dimension_semantics=sem, vmem_limit_bytes=VMEM_LIMIT)


def _tile(n, pref, unit):
    if n <= pref:
        return n
    t = (pref // unit) * unit
    while n % t:
        t -= unit
    return t


def _sigmoid(x):
    return 1.0 / (1.0 + jnp.exp(-x))


def _silu(x):
    return x * _sigmoid(x)


def _mod_kernel(c_ref, w_ref, b_ref, o_ref):
    s = _silu(c_ref[...]).astype(BF16)
    o_ref[...] = jnp.dot(s, w_ref[...].astype(BF16), preferred_element_type=F32) + b_ref[...]


def _mod_call(c8, w_mod, b_mod):
    depth, d, n3 = w_mod.shape
    tn = _tile(n3, 512, LANES)
    return pl.pallas_call(
        _mod_kernel,
        grid=(depth, n3 // tn),
        in_specs=[pl.BlockSpec((MOD_ROWS, d), lambda l, j: (0, 0)),
                  pl.BlockSpec((None, d, tn), lambda l, j: (l, 0, j)),
                  pl.BlockSpec((None, 1, tn), lambda l, j: (l, 0, j))],
        out_specs=pl.BlockSpec((None, MOD_ROWS, tn), lambda l, j: (l, 0, j)),
        out_shape=jax.ShapeDtypeStruct((depth, MOD_ROWS, n3), F32),
        compiler_params=_cparams("arbitrary", "arbitrary"),
        name="mod",
    )(c8, w_mod, b_mod.reshape(depth, 1, n3))


def _adaln_kernel(x_ref, sh_ref, sc_ref, nw_ref, o_ref, *, row):
    r = pl.program_id(0) if row is None else row
    x = x_ref[...]
    inv = lax.rsqrt(jnp.mean(x * x, axis=-1, keepdims=True) + NORM_EPS)
    xn = (x * inv) * nw_ref[...]
    o_ref[...] = (xn * (1.0 + sc_ref[pl.ds(r, 1), :]) + sh_ref[pl.ds(r, 1), :]).astype(o_ref.dtype)


def _adaln_call(x, shift, scale, nw, row):
    b, l, d = x.shape
    tr = _tile(l, 256, BF16_ROWS)
    return pl.pallas_call(
        functools.partial(_adaln_kernel, row=row),
        grid=(b, l // tr),
        in_specs=[pl.BlockSpec((None, tr, d), lambda i, j: (i, j, 0)),
                  pl.BlockSpec((MOD_ROWS, d), lambda i, j: (0, 0)),
                  pl.BlockSpec((MOD_ROWS, d), lambda i, j: (0, 0)),
                  pl.BlockSpec((1, d), lambda i, j: (0, 0))],
        out_specs=pl.BlockSpec((None, tr, d), lambda i, j: (i, j, 0)),
        out_shape=jax.ShapeDtypeStruct((b, l, d), BF16),
        compiler_params=_cparams("arbitrary", "arbitrary"),
        name="adaln",
    )(x, shift, scale, nw.reshape(1, d))


def _mm_kernel(a_ref, w_ref, o_ref):
    o_ref[...] = jnp.dot(a_ref[...], w_ref[...], preferred_element_type=F32).astype(o_ref.dtype)


def _mm_call(a, w, col0, ncols, out_dtype):
    m, k = a.shape
    tm = _tile(m, 1024, BF16_ROWS)
    tn = _tile(ncols, 512, LANES)
    assert col0 % tn == 0
    off = col0 // tn
    return pl.pallas_call(
        _mm_kernel,
        grid=(m // tm, ncols // tn),
        in_specs=[pl.BlockSpec((tm, k), lambda i, j: (i, 0)),
                  pl.BlockSpec((k, tn), lambda i, j: (0, j + off))],
        out_specs=pl.BlockSpec((tm, tn), lambda i, j: (i, j)),
        out_shape=jax.ShapeDtypeStruct((m, ncols), out_dtype),
        compiler_params=_cparams("arbitrary", "arbitrary"),
        name="matmul",
    )(a, w)


def _mm_resid_kernel(a_ref, w_ref, x_ref, g_ref, o_ref, *, row, tiles_per_batch):
    r = (pl.program_id(0) // tiles_per_batch) if row is None else row
    y = jnp.dot(a_ref[...], w_ref[...], preferred_element_type=F32)
    o_ref[...] = x_ref[...] + g_ref[pl.ds(r, 1), :] * y


def _mm_resid_call(a, w, x, gate, rows_per_batch, row):
    m, k = a.shape
    n = w.shape[1]
    tm = _tile(rows_per_batch, 1024, BF16_ROWS)
    tn = _tile(n, 512, LANES)
    return pl.pallas_call(
        functools.partial(_mm_resid_kernel, row=row, tiles_per_batch=rows_per_batch // tm),
        grid=(m // tm, n // tn),
        in_specs=[pl.BlockSpec((tm, k), lambda i, j: (i, 0)),
                  pl.BlockSpec((k, tn), lambda i, j: (0, j)),
                  pl.BlockSpec((tm, tn), lambda i, j: (i, j)),
                  pl.BlockSpec((MOD_ROWS, tn), lambda i, j: (0, j))],
        out_specs=pl.BlockSpec((tm, tn), lambda i, j: (i, j)),
        out_shape=jax.ShapeDtypeStruct((m, n), F32),
        compiler_params=_cparams("arbitrary", "arbitrary"),
        name="matmul_resid",
    )(a, w, x, gate)


def _scan_rows(a, b, h_in, reverse):
    t, w = a.shape
    g = t // SUBLANES
    av = a.reshape(g, SUBLANES, w)
    bv = b.reshape(g, SUBLANES, w)
    sub = lax.broadcasted_iota(jnp.int32, (g, SUBLANES, w), 1)
    for k in (1, 2, 4):
        shift = SUBLANES - k if reverse else k
        valid = (sub < SUBLANES - k) if reverse else (sub >= k)
        a_sh = pltpu.roll(av, shift, 1)
        b_sh = pltpu.roll(bv, shift, 1)
        bv = jnp.where(valid, av * b_sh + bv, bv)
        av = jnp.where(valid, av * a_sh, av)
    outs = [None] * g
    h = h_in
    for gi in (range(g - 1, -1, -1) if reverse else range(g)):
        hg = av[gi] * h + bv[gi]
        outs[gi] = hg
        h = hg[0:1] if reverse else hg[SUBLANES - 1:SUBLANES]
    return jnp.concatenate(outs, axis=0), h


def _rglru_kernel(ul_ref, gl_ref, uc_ref, gc_ref, cw_ref, cb_ref, wg_ref, bg_ref, lam_ref,
                  zl_ref, zc_ref, cu_ref, sf_ref, *, s_len, c_len, bw, t_lat, t_ctx):
    cw = cw_ref[...]
    cb = cb_ref[...]
    nlam = -lam_ref[...]
    softplus = jnp.maximum(nlam, 0.0) + jnp.log1p(jnp.exp(-jnp.abs(nlam)))
    cl = -LRU_C * softplus

    def conv_chunk(u_ref, t0, t, length):
        halo = BF16_ROWS
        cur = u_ref[pl.ds(t0, t), :].astype(F32)
        p0 = pl.multiple_of(jnp.maximum(t0 - halo, 0), halo)
        n0 = pl.multiple_of(jnp.minimum(t0 + t, length - halo), halo)
        prev = u_ref[pl.ds(p0, halo), :].astype(F32)
        nxt = u_ref[pl.ds(n0, halo), :].astype(F32)
        prev = jnp.where(t0 > 0, prev, 0.0)
        nxt = jnp.where(t0 + t < length, nxt, 0.0)
        ext = jnp.concatenate([prev, cur, nxt], axis=0)
        n = t + 2 * halo
        acc = cw[CONV_PAD_LO:CONV_PAD_LO + 1] * ext
        for k in range(CONV_W):
            if k != CONV_PAD_LO:
                acc = acc + cw[k:k + 1] * pltpu.roll(ext, (CONV_PAD_LO - k) % n, 0)
        return acc[halo:halo + t] + cb

    def coeffs(uc, d):
        lo, hi = 2 * d * bw, 2 * (d + 1) * bw
        zg = jnp.dot(uc.astype(BF16), wg_ref[:, lo:hi], preferred_element_type=F32) + bg_ref[:, lo:hi]
        r = _sigmoid(zg[:, :bw])
        i = _sigmoid(zg[:, bw:])
        log_a = cl[d:d + 1] * r
        a = jnp.exp(log_a)
        th = jnp.tanh(-log_a)
        b = jnp.sqrt(2.0 * th / (1.0 + th)) * (i * uc)
        return a, b

    def run_sequence(u_ref, g_ref, z_ref, length, t, hf, hb):
        n_chunks = length // t

        def conv_body(k, carry):
            t0 = pl.multiple_of(k * t, t)
            cu_ref[pl.ds(t0, t), :] = conv_chunk(u_ref, t0, t, length)
            return carry

        lax.fori_loop(0, n_chunks, conv_body, 0)

        def fwd_body(k, h):
            t0 = pl.multiple_of(k * t, t)
            a, b = coeffs(cu_ref[pl.ds(t0, t), :], 0)
            states, h = _scan_rows(a, b, h, False)
            sf_ref[pl.ds(t0, t), :] = states
            return h

        hf = lax.fori_loop(0, n_chunks, fwd_body, hf)

        def bwd_body(kk, h):
            t0 = pl.multiple_of((n_chunks - 1 - kk) * t, t)
            a, b = coeffs(cu_ref[pl.ds(t0, t), :], 1)
            states, h = _scan_rows(a, b, h, True)
            gate = _silu(g_ref[pl.ds(t0, t), :].astype(F32))
            z_ref[pl.ds(t0, t), :] = ((sf_ref[pl.ds(t0, t), :] + states) * gate).astype(z_ref.dtype)
            return h

        hb = lax.fori_loop(0, n_chunks, bwd_body, hb)
        return hf, hb

    zero = jnp.zeros((1, bw), F32)
    hf0, hb0 = run_sequence(uc_ref, gc_ref, zc_ref, c_len, t_ctx, zero, zero)
    run_sequence(ul_ref, gl_ref, zl_ref, s_len, t_lat, hf0, hb0)


def _rglru_call(ug_lat, ug_ctx, conv_w, conv_b, w_gate, b_gate, lam):
    b, s_len, r2 = ug_lat.shape
    c_len = ug_ctx.shape[1]
    r = r2 // 2
    nblk, bw, _ = w_gate.shape
    t_lat = _tile(s_len, 256, BF16_ROWS)
    t_ctx = _tile(c_len, 256, BF16_ROWS)
    kern = functools.partial(_rglru_kernel, s_len=s_len, c_len=c_len, bw=bw, t_lat=t_lat, t_ctx=t_ctx)
    return pl.pallas_call(
        kern,
        grid=(b, nblk),
        in_specs=[pl.BlockSpec((None, s_len, bw), lambda i, j: (i, 0, j)),
                  pl.BlockSpec((None, s_len, bw), lambda i, j: (i, 0, j + nblk)),
                  pl.BlockSpec((None, c_len, bw), lambda i, j: (i, 0, j)),
                  pl.BlockSpec((None, c_len, bw), lambda i, j: (i, 0, j + nblk)),
                  pl.BlockSpec((CONV_W, bw), lambda i, j: (0, j)),
                  pl.BlockSpec((1, bw), lambda i, j: (0, j)),
                  pl.BlockSpec((None, bw, 4 * bw), lambda i, j: (j, 0, 0)),
                  pl.BlockSpec((None, 1, 4 * bw), lambda i, j: (j, 0, 0)),
                  pl.BlockSpec((2, bw), lambda i, j: (0, j))],
        out_specs=[pl.BlockSpec((None, s_len, bw), lambda i, j: (i, 0, j)),
                   pl.BlockSpec((None, c_len, bw), lambda i, j: (i, 0, j))],
        out_shape=[jax.ShapeDtypeStruct((b, s_len, r), BF16),
                   jax.ShapeDtypeStruct((b, c_len, r), BF16)],
        scratch_shapes=[pltpu.VMEM((max(s_len, c_len), bw), F32),
                        pltpu.VMEM((max(s_len, c_len), bw), F32)],
        compiler_params=_cparams("arbitrary", "arbitrary"),
        name="rglru",
    )(ug_lat, ug_lat, ug_ctx, ug_ctx, conv_w, conv_b.reshape(1, r), w_gate, b_gate, lam)


def _qkprep_kernel(x_ref, cos_ref, sin_ref, w_ref, o_ref, *, n_heads, rope):
    w = w_ref[...]
    tr = x_ref.shape[0]
    if rope:
        cos = cos_ref[...]
        sin = sin_ref[...]
        lane = lax.broadcasted_iota(jnp.int32, (tr, HEAD_DIM), 1)
        first_half = (lane % (HEAD_DIM // 2)) < (HEAD_DIM // 4)
    for h in range(n_heads):
        sl = slice(h * HEAD_DIM, (h + 1) * HEAD_DIM)
        x = x_ref[:, sl].astype(F32)
        inv = lax.rsqrt(jnp.mean(x * x, axis=-1, keepdims=True) + NORM_EPS)
        xn = (x * inv) * w
        if rope:
            partner = jnp.where(first_half,
                                pltpu.roll(xn, HEAD_DIM - HEAD_DIM // 4, 1),
                                pltpu.roll(xn, HEAD_DIM // 4, 1))
            xn = xn * cos + partner * sin
        o_ref[:, sl] = xn.astype(o_ref.dtype)


def _qkprep_call(x, col0, width, cos, sin, w, rope):
    b, l, _ = x.shape
    tr = _tile(l, 256, BF16_ROWS)
    assert col0 % width == 0
    off = col0 // width
    return pl.pallas_call(
        functools.partial(_qkprep_kernel, n_heads=width // HEAD_DIM, rope=rope),
        grid=(b, l // tr),
        in_specs=[pl.BlockSpec((None, tr, width), lambda i, j: (i, j, off)),
                  pl.BlockSpec((tr, HEAD_DIM), lambda i, j: (j, 0)),
                  pl.BlockSpec((tr, HEAD_DIM), lambda i, j: (j, 0)),
                  pl.BlockSpec((1, HEAD_DIM), lambda i, j: (0, 0))],
        out_specs=pl.BlockSpec((None, tr, width), lambda i, j: (i, j, 0)),
        out_shape=jax.ShapeDtypeStruct((b, l, width), BF16),
        compiler_params=_cparams("arbitrary", "arbitrary"),
        name="qkprep",
    )(x, cos, sin, w.reshape(1, HEAD_DIM))


def _attn_kernel(sink_ref, q_ref, kp_ref, ko_ref, kn_ref, vp_ref, vo_ref, vn_ref, kc_ref, vc_ref, g_ref,
                 o_ref, *, groups, c_len, n_blocks):
    kvh = pl.program_id(1)
    qb = pl.program_id(2)
    blk = ATTN_BLOCK
    q = q_ref[...]
    qs = jnp.concatenate([q[:, g * HEAD_DIM:(g + 1) * HEAD_DIM] for g in range(groups)], axis=0)
    keys = jnp.concatenate([kc_ref[...], kp_ref[...], ko_ref[...], kn_ref[...]], axis=0)
    vals = jnp.concatenate([vc_ref[...], vp_ref[...], vo_ref[...], vn_ref[...]], axis=0)
    s = lax.dot_general(qs, keys, (((1,), (1,)), ((), ())), preferred_element_type=F32) * (HEAD_DIM ** -0.5)
    rows, nk = s.shape
    qi = lax.broadcasted_iota(jnp.int32, (rows, nk), 0) % blk
    col = lax.broadcasted_iota(jnp.int32, (rows, nk), 1)
    ki = col - c_len
    rel = ki - blk - qi
    ok = (jnp.abs(rel) <= WINDOW) & ((ki >= blk) | (qb > 0)) & ((ki < 2 * blk) | (qb < n_blocks - 1))
    s = jnp.where(ok | (col < c_len), s, NEG_INF)
    sink = jnp.concatenate([jnp.full((blk, 1), sink_ref[kvh * groups + g], F32) for g in range(groups)], axis=0)
    m = jnp.maximum(jnp.max(s, axis=-1, keepdims=True), sink)
    p = jnp.exp(s - m)
    den = jnp.sum(p, axis=-1, keepdims=True) + jnp.exp(sink - m)
    o = jnp.dot(p.astype(BF16), vals, preferred_element_type=F32) / den
    o = jnp.concatenate([o[g * blk:(g + 1) * blk] for g in range(groups)], axis=1)
    o_ref[...] = (o * _silu(g_ref[...].astype(F32))).astype(o_ref.dtype)


def _attn_call(sink, qn, kn, kcn, qkvg, kvc, attn_w, kv_w):
    b, s_len, _ = qn.shape
    c_len = kcn.shape[1]
    n_kv = kv_w // HEAD_DIM
    groups = attn_w // kv_w
    gw = groups * HEAD_DIM
    nb = s_len // ATTN_BLOCK
    v_off = (attn_w + kv_w) // HEAD_DIM
    g_off = (attn_w + 2 * kv_w) // gw
    assert (attn_w + 2 * kv_w) % gw == 0
    blk = ATTN_BLOCK

    def kv_spec(shift, off):
        return pl.BlockSpec((None, blk, HEAD_DIM),
                            lambda i, h, j: (i, jnp.clip(j + shift, 0, nb - 1), h + off))

    return pl.pallas_call(
        functools.partial(_attn_kernel, groups=groups, c_len=c_len, n_blocks=nb),
        grid=(b, n_kv, nb),
        in_specs=[pl.BlockSpec(memory_space=pltpu.SMEM),
                  pl.BlockSpec((None, blk, gw), lambda i, h, j: (i, j, h)),
                  kv_spec(-1, 0), kv_spec(0, 0), kv_spec(1, 0),
                  kv_spec(-1, v_off), kv_spec(0, v_off), kv_spec(1, v_off),
                  pl.BlockSpec((None, c_len, HEAD_DIM), lambda i, h, j: (i, 0, h)),
                  pl.BlockSpec((None, c_len, HEAD_DIM), lambda i, h, j: (i, 0, h + n_kv)),
                  pl.BlockSpec((None, blk, gw), lambda i, h, j: (i, j, h + g_off))],
        out_specs=pl.BlockSpec((None, blk, gw), lambda i, h, j: (i, j, h)),
        out_shape=jax.ShapeDtypeStruct((b, s_len, attn_w), BF16),
        compiler_params=_cparams("arbitrary", "arbitrary", "arbitrary"),
        name="attention",
    )(sink, qn, kn, kn, kn, qkvg, qkvg, qkvg, kcn, kvc, qkvg)


def _rope_tables(s_len):
    t = jnp.arange(s_len, dtype=jnp.int32)
    pos = jnp.stack([t // GRID_W, t % GRID_W], axis=1).astype(F32)
    half = HEAD_DIM // 2
    quarter = half // 2
    inv_freq = ROPE_BASE ** (-jnp.arange(quarter, dtype=F32) * (2.0 / half))
    ang = pos[:, :, None] * inv_freq[None, None, :]
    cos = jnp.concatenate([jnp.cos(ang), jnp.cos(ang)], axis=-1).reshape(s_len, HEAD_DIM)
    sin = jnp.concatenate([-jnp.sin(ang), jnp.sin(ang)], axis=-1).reshape(s_len, HEAD_DIM)
    return cos, sin


def kernel(x, c, ctx, c_ctx, w_mod, b_mod, norm_w, rg_w_in, rg_conv_w, rg_conv_b, rg_w_r, rg_b_r, rg_w_i,
           rg_b_i, rg_lam, rg_w_out, at_w_in, at_q_norm, at_k_norm, at_sink, at_w_out):
    b, s_len, d = x.shape
    c_len = ctx.shape[1]
    depth = w_mod.shape[0]
    assert depth == 2 and b < MOD_ROWS
    ctx_row = b

    c8 = jnp.concatenate([c, c_ctx[None, :], jnp.zeros((MOD_ROWS - b - 1, d), F32)], axis=0)
    mod = _mod_call(c8, w_mod, b_mod)
    shift, scale, gate = mod[:, :, :d], mod[:, :, d:2 * d], mod[:, :, 2 * d:]

    r = rg_w_out.shape[1]
    h = _adaln_call(x, shift[0], scale[0], norm_w[0], None)
    hc = _adaln_call(ctx, shift[0], scale[0], norm_w[0], ctx_row)
    w_in = rg_w_in[0].astype(BF16)
    ug = _mm_call(h.reshape(b * s_len, d), w_in, 0, 2 * r, BF16).reshape(b, s_len, 2 * r)
    ugc = _mm_call(hc.reshape(b * c_len, d), w_in, 0, 2 * r, BF16).reshape(b, c_len, 2 * r)
    w_gate = jnp.concatenate([rg_w_r[0, 0], rg_w_i[0, 0], rg_w_r[0, 1], rg_w_i[0, 1]], axis=-1).astype(BF16)
    nblk, bw = rg_w_r.shape[2], rg_w_r.shape[3]
    b_gate = jnp.concatenate([rg_b_r[0, 0].reshape(nblk, 1, bw), rg_b_i[0, 0].reshape(nblk, 1, bw),
                              rg_b_r[0, 1].reshape(nblk, 1, bw), rg_b_i[0, 1].reshape(nblk, 1, bw)], axis=-1)
    z, zc = _rglru_call(ug, ugc, rg_conv_w[0], rg_conv_b[0], w_gate, b_gate, rg_lam[0])
    w_out = rg_w_out[0].astype(BF16)
    x = _mm_resid_call(z.reshape(b * s_len, r), w_out, x.reshape(b * s_len, d), gate[0], s_len, None)
    x = x.reshape(b, s_len, d)
    ctx = _mm_resid_call(zc.reshape(b * c_len, r), w_out, ctx.reshape(b * c_len, d), gate[0], c_len, ctx_row)
    ctx = ctx.reshape(b, c_len, d)

    attn_w = at_w_out.shape[1]
    kv_w = (at_w_in.shape[2] - 2 * attn_w) // 2
    h = _adaln_call(x, shift[1], scale[1], norm_w[1], None)
    hc = _adaln_call(ctx, shift[1], scale[1], norm_w[1], ctx_row)
    w_in = at_w_in[0].astype(BF16)
    qkvg = _mm_call(h.reshape(b * s_len, d), w_in, 0, w_in.shape[1], BF16).reshape(b, s_len, -1)
    kvc = _mm_call(hc.reshape(b * c_len, d), w_in, attn_w, 2 * kv_w, BF16).reshape(b, c_len, 2 * kv_w)
    cos, sin = _rope_tables(s_len)
    qn = _qkprep_call(qkvg, 0, attn_w, cos, sin, at_q_norm[0], True)
    kn = _qkprep_call(qkvg, attn_w, kv_w, cos, sin, at_k_norm[0], True)
    kcn = _qkprep_call(kvc, 0, kv_w, cos[:c_len], sin[:c_len], at_k_norm[0], False)
    z = _attn_call(at_sink[0], qn, kn, kcn, qkvg, kvc, attn_w, kv_w)
    x = _mm_resid_call(z.reshape(b * s_len, attn_w), at_w_out[0].astype(BF16), x.reshape(b * s_len, d),
                       gate[1], s_len, None)
    return x.reshape(b, s_len, d)
```

```python
import functools

import jax
import jax.numpy as jnp
from jax import lax
from jax.experimental import pallas as pl
from jax.experimental.pallas import tpu as pltpu

F32 = jnp.float32
BF16 = jnp.bfloat16

HEAD_DIM = 128
WINDOW = 128
ATTN_BLOCK = 128
GRID_W = 64
ROPE_BASE = 10000.0
NORM_EPS = 1e-6
NEG_INF = -1e30
LRU_C = 8.0
CONV_W = 4
CONV_PAD_LO = 2
LOG2E = 1.4426950408889634

SUBLANES = 8
LANES = 128
BF16_ROWS = 16
MOD_ROWS = 8
VMEM_LIMIT = 56 * 1024 * 1024


def _cparams(*sem):
    return pltpu.CompilerParams(dimension_semantics=sem, vmem_limit_bytes=VMEM_LIMIT)


def _tile(n, pref, unit):
    if n <= pref:
        return n
    t = (pref // unit) * unit
    while n % t:
        t -= unit
    return t


def _sigmoid(x):
    return 1.0 / (1.0 + jnp.exp(-x))


def _silu(x):
    return x * _sigmoid(x)


def _mod_kernel(c_ref, w_ref, b_ref, o_ref):
    s = _silu(c_ref[...]).astype(BF16)
    o_ref[...] = jnp.dot(s, w_ref[...].astype(BF16), preferred_element_type=F32) + b_ref[...]


def _mod_call(c8, w_mod, b_mod):
    depth, d, n3 = w_mod.shape
    tn = _tile(n3, 512, LANES)
    return pl.pallas_call(
        _mod_kernel,
        grid=(depth, n3 // tn),
        in_specs=[pl.BlockSpec((MOD_ROWS, d), lambda l, j: (0, 0)),
                  pl.BlockSpec((None, d, tn), lambda l, j: (l, 0, j)),
                  pl.BlockSpec((None, 1, tn), lambda l, j: (l, 0, j))],
        out_specs=pl.BlockSpec((None, MOD_ROWS, tn), lambda l, j: (l, 0, j)),
        out_shape=jax.ShapeDtypeStruct((depth, MOD_ROWS, n3), F32),
        compiler_params=_cparams("arbitrary", "arbitrary"),
        name="mod",
    )(c8, w_mod, b_mod.reshape(depth, 1, n3))


def _adaln_kernel(x_ref, sh_ref, sc_ref, nw_ref, o_ref, *, row):
    r = pl.program_id(0) if row is None else row
    x = x_ref[...]
    inv = lax.rsqrt(jnp.mean(x * x, axis=-1, keepdims=True) + NORM_EPS)
    xn = (x * inv) * nw_ref[...]
    o_ref[...] = (xn * (1.0 + sc_ref[pl.ds(r, 1), :]) + sh_ref[pl.ds(r, 1), :]).astype(o_ref.dtype)


def _adaln_call(x, shift, scale, nw, row):
    b, l, d = x.shape
    tr = _tile(l, 256, BF16_ROWS)
    return pl.pallas_call(
        functools.partial(_adaln_kernel, row=row),
        grid=(b, l // tr),
        in_specs=[pl.BlockSpec((None, tr, d), lambda i, j: (i, j, 0)),
                  pl.BlockSpec((MOD_ROWS, d), lambda i, j: (0, 0)),
                  pl.BlockSpec((MOD_ROWS, d), lambda i, j: (0, 0)),
                  pl.BlockSpec((1, d), lambda i, j: (0, 0))],
        out_specs=pl.BlockSpec((None, tr, d), lambda i, j: (i, j, 0)),
        out_shape=jax.ShapeDtypeStruct((b, l, d), BF16),
        compiler_params=_cparams("arbitrary", "arbitrary"),
        name="adaln",
    )(x, shift, scale, nw.reshape(1, d))


def _mm_kernel(a_ref, w_ref, o_ref):
    o_ref[...] = jnp.dot(a_ref[...], w_ref[...], preferred_element_type=F32).astype(o_ref.dtype)


def _mm_call(a, w, col0, ncols, out_dtype):
    m, k = a.shape
    tm = _tile(m, 1024, BF16_ROWS)
    tn = _tile(ncols, 512, LANES)
    assert col0 % tn == 0
    off = col0 // tn
    return pl.pallas_call(
        _mm_kernel,
        grid=(m // tm, ncols // tn),
        in_specs=[pl.BlockSpec((tm, k), lambda i, j: (i, 0)),
                  pl.BlockSpec((k, tn), lambda i, j: (0, j + off))],
        out_specs=pl.BlockSpec((tm, tn), lambda i, j: (i, j)),
        out_shape=jax.ShapeDtypeStruct((m, ncols), out_dtype),
        compiler_params=_cparams("arbitrary", "arbitrary"),
        name="matmul",
    )(a, w)


def _mm_resid_kernel(a_ref, w_ref, x_ref, g_ref, o_ref, *, row, tiles_per_batch):
    r = (pl.program_id(0) // tiles_per_batch) if row is None else row
    y = jnp.dot(a_ref[...], w_ref[...], preferred_element_type=F32)
    o_ref[...] = x_ref[...] + g_ref[pl.ds(r, 1), :] * y


def _mm_resid_call(a, w, x, gate, rows_per_batch, row):
    m, k = a.shape
    n = w.shape[1]
    tm = _tile(rows_per_batch, 1024, BF16_ROWS)
    tn = _tile(n, 512, LANES)
    return pl.pallas_call(
        functools.partial(_mm_resid_kernel, row=row, tiles_per_batch=rows_per_batch // tm),
        grid=(m // tm, n // tn),
        in_specs=[pl.BlockSpec((tm, k), lambda i, j: (i, 0)),
                  pl.BlockSpec((k, tn), lambda i, j: (0, j)),
                  pl.BlockSpec((tm, tn), lambda i, j: (i, j)),
                  pl.BlockSpec((MOD_ROWS, tn), lambda i, j: (0, j))],
        out_specs=pl.BlockSpec((tm, tn), lambda i, j: (i, j)),
        out_shape=jax.ShapeDtypeStruct((m, n), F32),
        compiler_params=_cparams("arbitrary", "arbitrary"),
        name="matmul_resid",
    )(a, w, x, gate)


def _scan_rows(a, b, h_in, reverse):
    t, w = a.shape
    g = t // SUBLANES
    av = a.reshape(g, SUBLANES, w)
    bv = b.reshape(g, SUBLANES, w)
    sub = lax.broadcasted_iota(jnp.int32, (g, SUBLANES, w), 1)
    for k in (1, 2, 4):
        shift = SUBLANES - k if reverse else k
        valid = (sub < SUBLANES - k) if reverse else (sub >= k)
        a_sh = pltpu.roll(av, shift, 1)
        b_sh = pltpu.roll(bv, shift, 1)
        bv = jnp.where(valid, av * b_sh + bv, bv)
        av = jnp.where(valid, av * a_sh, av)
    outs = [None] * g
    h = h_in
    for gi in (range(g - 1, -1, -1) if reverse else range(g)):
        hg = av[gi] * h + bv[gi]
        outs[gi] = hg
        h = hg[0:1] if reverse else hg[SUBLANES - 1:SUBLANES]
    return jnp.concatenate(outs, axis=0), h


def _rglru_kernel(ul_ref, gl_ref, uc_ref, gc_ref, cw_ref, cb_ref, wg_ref, bg_ref, lam_ref,
                  zl_ref, zc_ref, cu_ref, sf_ref, *, s_len, c_len, bw, t_lat, t_ctx):
    cw = cw_ref[...]
    cb = cb_ref[...]
    nlam = -lam_ref[...]
    softplus = jnp.maximum(nlam, 0.0) + jnp.log1p(jnp.exp(-jnp.abs(nlam)))
    cl = -LRU_C * softplus

    def conv_chunk(u_ref, t0, t, length):
        halo = BF16_ROWS
        cur = u_ref[pl.ds(t0, t), :].astype(F32)
        p0 = pl.multiple_of(jnp.maximum(t0 - halo, 0), halo)
        n0 = pl.multiple_of(jnp.minimum(t0 + t, length - halo), halo)
        prev = u_ref[pl.ds(p0, halo), :].astype(F32)
        nxt = u_ref[pl.ds(n0, halo), :].astype(F32)
        prev = jnp.where(t0 > 0, prev, 0.0)
        nxt = jnp.where(t0 + t < length, nxt, 0.0)
        ext = jnp.concatenate([prev, cur, nxt], axis=0)
        n = t + 2 * halo
        acc = cw[CONV_PAD_LO:CONV_PAD_LO + 1] * ext
        for k in range(CONV_W):
            if k != CONV_PAD_LO:
                acc = acc + cw[k:k + 1] * pltpu.roll(ext, (CONV_PAD_LO - k) % n, 0)
        return acc[halo:halo + t] + cb

    def coeffs(uc, d):
        lo, hi = 2 * d * bw, 2 * (d + 1) * bw
        zg = jnp.dot(uc.astype(BF16), wg_ref[:, lo:hi], preferred_element_type=F32) + bg_ref[:, lo:hi]
        r = _sigmoid(zg[:, :bw])
        i = _sigmoid(zg[:, bw:])
        log_a = cl[d:d + 1] * r
        a = jnp.exp(log_a)
        th = jnp.tanh(-log_a)
        b = jnp.sqrt(2.0 * th / (1.0 + th)) * (i * uc)
        return a, b

    def run_sequence(u_ref, g_ref, z_ref, length, t, hf, hb):
        n_chunks = length // t

        def conv_body(k, carry):
            t0 = pl.multiple_of(k * t, t)
            cu_ref[pl.ds(t0, t), :] = conv_chunk(u_ref, t0, t, length)
            return carry

        lax.fori_loop(0, n_chunks, conv_body, 0)

        def fwd_body(k, h):
            t0 = pl.multiple_of(k * t, t)
            a, b = coeffs(cu_ref[pl.ds(t0, t), :], 0)
            states, h = _scan_rows(a, b, h, False)
            sf_ref[pl.ds(t0, t), :] = states
            return h

        hf = lax.fori_loop(0, n_chunks, fwd_body, hf)

        def bwd_body(kk, h):
            t0 = pl.multiple_of((n_chunks - 1 - kk) * t, t)
            a, b = coeffs(cu_ref[pl.ds(t0, t), :], 1)
            states, h = _scan_rows(a, b, h, True)
            gate = _silu(g_ref[pl.ds(t0, t), :].astype(F32))
            z_ref[pl.ds(t0, t), :] = ((sf_ref[pl.ds(t0, t), :] + states) * gate).astype(z_ref.dtype)
            return h

        hb = lax.fori_loop(0, n_chunks, bwd_body, hb)
        return hf, hb

    zero = jnp.zeros((1, bw), F32)
    hf0, hb0 = run_sequence(uc_ref, gc_ref, zc_ref, c_len, t_ctx, zero, zero)
    run_sequence(ul_ref, gl_ref, zl_ref, s_len, t_lat, hf0, hb0)


def _rglru_call(ug_lat, ug_ctx, conv_w, conv_b, w_gate, b_gate, lam):
    b, s_len, r2 = ug_lat.shape
    c_len = ug_ctx.shape[1]
    r = r2 // 2
    nblk, bw, _ = w_gate.shape
    t_lat = _tile(s_len, 256, BF16_ROWS)
    t_ctx = _tile(c_len, 256, BF16_ROWS)
    kern = functools.partial(_rglru_kernel, s_len=s_len, c_len=c_len, bw=bw, t_lat=t_lat, t_ctx=t_ctx)
    return pl.pallas_call(
        kern,
        grid=(b, nblk),
        in_specs=[pl.BlockSpec((None, s_len, bw), lambda i, j: (i, 0, j)),
                  pl.BlockSpec((None, s_len, bw), lambda i, j: (i, 0, j + nblk)),
                  pl.BlockSpec((None, c_len, bw), lambda i, j: (i, 0, j)),
                  pl.BlockSpec((None, c_len, bw), lambda i, j: (i, 0, j + nblk)),
                  pl.BlockSpec((CONV_W, bw), lambda i, j: (0, j)),
                  pl.BlockSpec((1, bw), lambda i, j: (0, j)),
                  pl.BlockSpec((None, bw, 4 * bw), lambda i, j: (j, 0, 0)),
                  pl.BlockSpec((None, 1, 4 * bw), lambda i, j: (j, 0, 0)),
                  pl.BlockSpec((2, bw), lambda i, j: (0, j))],
        out_specs=[pl.BlockSpec((None, s_len, bw), lambda i, j: (i, 0, j)),
                   pl.BlockSpec((None, c_len, bw), lambda i, j: (i, 0, j))],
        out_shape=[jax.ShapeDtypeStruct((b, s_len, r), BF16),
                   jax.ShapeDtypeStruct((b, c_len, r), BF16)],
        scratch_shapes=[pltpu.VMEM((max(s_len, c_len), bw), F32),
                        pltpu.VMEM((max(s_len, c_len), bw), F32)],
        compiler_params=_cparams("arbitrary", "arbitrary"),
        name="rglru",
    )(ug_lat, ug_lat, ug_ctx, ug_ctx, conv_w, conv_b.reshape(1, r), w_gate, b_gate, lam)


def _qkprep_kernel(x_ref, cos_ref, sin_ref, w_ref, o_ref, *, n_heads, rope):
    w = w_ref[...]
    tr = x_ref.shape[0]
    if rope:
        cos = cos_ref[...]
        sin = sin_ref[...]
        lane = lax.broadcasted_iota(jnp.int32, (tr, HEAD_DIM), 1)
        first_half = (lane % (HEAD_DIM // 2)) < (HEAD_DIM // 4)
    for h in range(n_heads):
        sl = slice(h * HEAD_DIM, (h + 1) * HEAD_DIM)
        x = x_ref[:, sl].astype(F32)
        inv = lax.rsqrt(jnp.mean(x * x, axis=-1, keepdims=True) + NORM_EPS)
        xn = (x * inv) * w
        if rope:
            partner = jnp.where(first_half,
                                pltpu.roll(xn, HEAD_DIM - HEAD_DIM // 4, 1),
                                pltpu.roll(xn, HEAD_DIM // 4, 1))
            xn = xn * cos + partner * sin
        o_ref[:, sl] = xn.astype(o_ref.dtype)


def _qkprep_call(x, col0, width, cos, sin, w, rope):
    b, l, _ = x.shape
    tr = _tile(l, 256, BF16_ROWS)
    assert col0 % width == 0
    off = col0 // width
    return pl.pallas_call(
        functools.partial(_qkprep_kernel, n_heads=width // HEAD_DIM, rope=rope),
        grid=(b, l // tr),
        in_specs=[pl.BlockSpec((None, tr, width), lambda i, j: (i, j, off)),
                  pl.BlockSpec((tr, HEAD_DIM), lambda i, j: (j, 0)),
                  pl.BlockSpec((tr, HEAD_DIM), lambda i, j: (j, 0)),
                  pl.BlockSpec((1, HEAD_DIM), lambda i, j: (0, 0))],
        out_specs=pl.BlockSpec((None, tr, width), lambda i, j: (i, j, 0)),
        out_shape=jax.ShapeDtypeStruct((b, l, width), BF16),
        compiler_params=_cparams("arbitrary", "arbitrary"),
        name="qkprep",
    )(x, cos, sin, w.reshape(1, HEAD_DIM))


def _dot_nt(a, b):
    return lax.dot_general(a, b, (((1,), (1,)), ((), ())), preferred_element_type=F32)


def _attn_kernel(sink_ref, bias_ref, q_ref, kp_ref, km_ref, kn_ref, vp_ref, vm_ref, vn_ref, kc_ref, vc_ref,
                 g_ref, o_ref, *, groups, qblocks, n_steps):
    kvh = pl.program_id(1)
    step = pl.program_id(2)
    blk = ATTN_BLOCK
    kw = jnp.concatenate([kp_ref[...], km_ref[...], kn_ref[...]], axis=0)
    vw = jnp.concatenate([vp_ref[...], vm_ref[...], vn_ref[...]], axis=0)
    kc = kc_ref[...]
    vt_c = jnp.concatenate([vc_ref[...].T, jnp.ones((BF16_ROWS, kc.shape[0]), BF16)], axis=0)
    vt_w = jnp.concatenate([vw.T, jnp.ones((BF16_ROWS, vw.shape[0]), BF16)], axis=0)
    c2 = (HEAD_DIM ** -0.5) * LOG2E
    sink2 = jnp.concatenate([jnp.full((1, blk), sink_ref[kvh * groups + g] * LOG2E, F32)
                             for g in range(groups)], axis=1)
    for i in range(qblocks):
        q = q_ref[i * blk:(i + 1) * blk, :]
        qs = jnp.concatenate([q[:, g * HEAD_DIM:(g + 1) * HEAD_DIM] for g in range(groups)], axis=0)
        first = (step == 0).astype(jnp.int32) if i == 0 else 0
        last = (step == n_steps - 1).astype(jnp.int32) if i == qblocks - 1 else 0
        bias = bias_ref[first + 2 * last]
        s_c = _dot_nt(kc, qs) * c2
        s_w = _dot_nt(kw[i * blk:(i + 3) * blk], qs) * c2 + bias
        m = jnp.maximum(jnp.maximum(jnp.max(s_c, axis=0, keepdims=True),
                                    jnp.max(s_w, axis=0, keepdims=True)), sink2)
        p_c = jnp.exp2(s_c - m).astype(BF16)
        p_w = jnp.exp2(s_w - m).astype(BF16)
        ot = (jnp.dot(vt_c, p_c, preferred_element_type=F32)
              + jnp.dot(vt_w[:, i * blk:(i + 3) * blk], p_w, preferred_element_type=F32))
        den = ot[HEAD_DIM:HEAD_DIM + 1, :] + jnp.exp2(sink2 - m)
        ot = ot[:HEAD_DIM, :] * (1.0 / den)
        o = jnp.concatenate([ot[:, g * blk:(g + 1) * blk].T for g in range(groups)], axis=1)
        gate = _silu(g_ref[i * blk:(i + 1) * blk, :].astype(F32))
        o_ref[i * blk:(i + 1) * blk, :] = (o * gate).astype(o_ref.dtype)


def _band_bias(groups):
    blk = ATTN_BLOCK
    ki = jnp.arange(3 * blk)[:, None]
    qi = jnp.arange(blk)[None, :]
    band = jnp.abs(ki - blk - qi) <= WINDOW
    out = []
    for var in range(4):
        ok = band & ((ki >= blk) | (var % 2 == 0)) & ((ki < 2 * blk) | (var // 2 == 0))
        out.append(jnp.tile(jnp.where(ok, 0.0, NEG_INF).astype(F32), (1, groups)))
    return jnp.stack(out)


def _attn_call(sink, qn, kn, kcn, qkvg, kvc, attn_w, kv_w):
    b, s_len, _ = qn.shape
    c_len = kcn.shape[1]
    n_kv = kv_w // HEAD_DIM
    groups = attn_w // kv_w
    gw = groups * HEAD_DIM
    blk = ATTN_BLOCK
    nb = s_len // blk
    qblocks = 4 if nb % 4 == 0 else 1
    n_steps = nb // qblocks
    v_off = (attn_w + kv_w) // HEAD_DIM
    g_off = (attn_w + 2 * kv_w) // gw
    assert (attn_w + 2 * kv_w) % gw == 0

    def edge_spec(prev, off):
        if prev:
            return pl.BlockSpec((None, blk, HEAD_DIM),
                                lambda i, h, j: (i, jnp.maximum(j * qblocks - 1, 0), h + off))
        return pl.BlockSpec((None, blk, HEAD_DIM),
                            lambda i, h, j: (i, jnp.minimum((j + 1) * qblocks, nb - 1), h + off))

    def main_spec(off):
        return pl.BlockSpec((None, qblocks * blk, HEAD_DIM), lambda i, h, j: (i, j, h + off))

    def tile_spec(off):
        return pl.BlockSpec((None, qblocks * blk, gw), lambda i, h, j: (i, j, h + off))

    return pl.pallas_call(
        functools.partial(_attn_kernel, groups=groups, qblocks=qblocks, n_steps=n_steps),
        grid=(b, n_kv, n_steps),
        in_specs=[pl.BlockSpec(memory_space=pltpu.SMEM),
                  pl.BlockSpec((4, 3 * blk, groups * blk), lambda i, h, j: (0, 0, 0)),
                  tile_spec(0),
                  edge_spec(True, 0), main_spec(0), edge_spec(False, 0),
                  edge_spec(True, v_off), main_spec(v_off), edge_spec(False, v_off),
                  pl.BlockSpec((None, c_len, HEAD_DIM), lambda i, h, j: (i, 0, h)),
                  pl.BlockSpec((None, c_len, HEAD_DIM), lambda i, h, j: (i, 0, h + n_kv)),
                  tile_spec(g_off)],
        out_specs=tile_spec(0),
        out_shape=jax.ShapeDtypeStruct((b, s_len, attn_w), BF16),
        compiler_params=_cparams("arbitrary", "arbitrary", "arbitrary"),
        name="attention",
    )(sink, _band_bias(groups), qn, kn, kn, kn, qkvg, qkvg, qkvg, kcn, kvc, qkvg)


def _rope_tables(s_len):
    t = jnp.arange(s_len, dtype=jnp.int32)
    pos = jnp.stack([t // GRID_W, t % GRID_W], axis=1).astype(F32)
    half = HEAD_DIM // 2
    quarter = half // 2
    inv_freq = ROPE_BASE ** (-jnp.arange(quarter, dtype=F32) * (2.0 / half))
    ang = pos[:, :, None] * inv_freq[None, None, :]
    cos = jnp.concatenate([jnp.cos(ang), jnp.cos(ang)], axis=-1).reshape(s_len, HEAD_DIM)
    sin = jnp.concatenate([-jnp.sin(ang), jnp.sin(ang)], axis=-1).reshape(s_len, HEAD_DIM)
    return cos, sin


def kernel(x, c, ctx, c_ctx, w_mod, b_mod, norm_w, rg_w_in, rg_conv_w, rg_conv_b, rg_w_r, rg_b_r, rg_w_i,
           rg_b_i, rg_lam, rg_w_out, at_w_in, at_q_norm, at_k_norm, at_sink, at_w_out):
    b, s_len, d = x.shape
    c_len = ctx.shape[1]
    depth = w_mod.shape[0]
    assert depth == 2 and b < MOD_ROWS
    ctx_row = b

    c8 = jnp.concatenate([c, c_ctx[None, :], jnp.zeros((MOD_ROWS - b - 1, d), F32)], axis=0)
    mod = _mod_call(c8, w_mod, b_mod)
    shift, scale, gate = mod[:, :, :d], mod[:, :, d:2 * d], mod[:, :, 2 * d:]

    r = rg_w_out.shape[1]
    h = _adaln_call(x, shift[0], scale[0], norm_w[0], None)
    hc = _adaln_call(ctx, shift[0], scale[0], norm_w[0], ctx_row)
    w_in = rg_w_in[0].astype(BF16)
    ug = _mm_call(h.reshape(b * s_len, d), w_in, 0, 2 * r, BF16).reshape(b, s_len, 2 * r)
    ugc = _mm_call(hc.reshape(b * c_len, d), w_in, 0, 2 * r, BF16).reshape(b, c_len, 2 * r)
    w_gate = jnp.concatenate([rg_w_r[0, 0], rg_w_i[0, 0], rg_w_r[0, 1], rg_w_i[0, 1]], axis=-1).astype(BF16)
    nblk, bw = rg_w_r.shape[2], rg_w_r.shape[3]
    b_gate = jnp.concatenate([rg_b_r[0, 0].reshape(nblk, 1, bw), rg_b_i[0, 0].reshape(nblk, 1, bw),
                              rg_b_r[0, 1].reshape(nblk, 1, bw), rg_b_i[0, 1].reshape(nblk, 1, bw)], axis=-1)
    z, zc = _rglru_call(ug, ugc, rg_conv_w[0], rg_conv_b[0], w_gate, b_gate, rg_lam[0])
    w_out = rg_w_out[0].astype(BF16)
    x = _mm_resid_call(z.reshape(b * s_len, r), w_out, x.reshape(b * s_len, d), gate[0], s_len, None)
    x = x.reshape(b, s_len, d)
    ctx = _mm_resid_call(zc.reshape(b * c_len, r), w_out, ctx.reshape(b * c_len, d), gate[0], c_len, ctx_row)
    ctx = ctx.reshape(b, c_len, d)

    attn_w = at_w_out.shape[1]
    kv_w = (at_w_in.shape[2] - 2 * attn_w) // 2
    h = _adaln_call(x, shift[1], scale[1], norm_w[1], None)
    hc = _adaln_call(ctx, shift[1], scale[1], norm_w[1], ctx_row)
    w_in = at_w_in[0].astype(BF16)
    qkvg = _mm_call(h.reshape(b * s_len, d), w_in, 0, w_in.shape[1], BF16).reshape(b, s_len, -1)
    kvc = _mm_call(hc.reshape(b * c_len, d), w_in, attn_w, 2 * kv_w, BF16).reshape(b, c_len, 2 * kv_w)
    cos, sin = _rope_tables(s_len)
    qn = _qkprep_call(qkvg, 0, attn_w, cos, sin, at_q_norm[0], True)
    kn = _qkprep_call(qkvg, attn_w, kv_w, cos, sin, at_k_norm[0], True)
    kcn = _qkprep_call(kvc, 0, kv_w, cos[:c_len], sin[:c_len], at_k_norm[0], False)
    z = _attn_call(at_sink[0], qn, kn, kcn, qkvg, kvc, attn_w, kv_w)
    x = _mm_resid_call(z.reshape(b * s_len, attn_w), at_w_out[0].astype(BF16), x.reshape(b * s_len, d),
                       gate[1], s_len, None)
    return x.reshape(b, s_len, d)
```

```python
import functools

import jax
import jax.numpy as jnp
from jax import lax
from jax.experimental import pallas as pl
from jax.experimental.pallas import tpu as pltpu

F32 = jnp.float32
BF16 = jnp.bfloat16

HEAD_DIM = 128
WINDOW = 128
ATTN_BLOCK = 128
GRID_W = 64
ROPE_BASE = 10000.0
NORM_EPS = 1e-6
NEG_INF = -1e30
LRU_C = 8.0
CONV_W = 4
CONV_PAD_LO = 2
LOG2E = 1.4426950408889634
LN2 = 0.6931471805599453
SQRT2 = 1.4142135623730951

SUBLANES = 8
LANES = 128
BF16_ROWS = 16
MOD_ROWS = 8
VMEM_LIMIT = 56 * 1024 * 1024
SCAN_STEPS = 64


def _cparams(*sem):
    return pltpu.CompilerParams(dimension_semantics=sem, vmem_limit_bytes=VMEM_LIMIT)


def _tile(n, pref, unit):
    if n <= pref:
        return n
    t = (pref // unit) * unit
    while n % t:
        t -= unit
    return t


def _sigmoid(x):
    return 1.0 / (1.0 + jnp.exp(-x))


def _silu(x):
    return x * _sigmoid(x)


def _mod_kernel(c_ref, w_ref, b_ref, o_ref):
    s = _silu(c_ref[...]).astype(BF16)
    o_ref[...] = jnp.dot(s, w_ref[...].astype(BF16), preferred_element_type=F32) + b_ref[...]


def _mod_call(c8, w_mod, b_mod):
    depth, d, n3 = w_mod.shape
    tn = _tile(n3, 512, LANES)
    return pl.pallas_call(
        _mod_kernel,
        grid=(depth, n3 // tn),
        in_specs=[pl.BlockSpec((MOD_ROWS, d), lambda l, j: (0, 0)),
                  pl.BlockSpec((None, d, tn), lambda l, j: (l, 0, j)),
                  pl.BlockSpec((None, 1, tn), lambda l, j: (l, 0, j))],
        out_specs=pl.BlockSpec((None, MOD_ROWS, tn), lambda l, j: (l, 0, j)),
        out_shape=jax.ShapeDtypeStruct((depth, MOD_ROWS, n3), F32),
        compiler_params=_cparams("arbitrary", "arbitrary"),
        name="mod",
    )(c8, w_mod, b_mod.reshape(depth, 1, n3))


def _adaln_kernel(x_ref, sh_ref, sc_ref, nw_ref, o_ref, *, row):
    r = pl.program_id(0) if row is None else row
    x = x_ref[...]
    inv = lax.rsqrt(jnp.mean(x * x, axis=-1, keepdims=True) + NORM_EPS)
    xn = (x * inv) * nw_ref[...]
    o_ref[...] = (xn * (1.0 + sc_ref[pl.ds(r, 1), :]) + sh_ref[pl.ds(r, 1), :]).astype(o_ref.dtype)


def _adaln_call(x, shift, scale, nw, row):
    b, l, d = x.shape
    tr = _tile(l, 256, BF16_ROWS)
    return pl.pallas_call(
        functools.partial(_adaln_kernel, row=row),
        grid=(b, l // tr),
        in_specs=[pl.BlockSpec((None, tr, d), lambda i, j: (i, j, 0)),
                  pl.BlockSpec((MOD_ROWS, d), lambda i, j: (0, 0)),
                  pl.BlockSpec((MOD_ROWS, d), lambda i, j: (0, 0)),
                  pl.BlockSpec((1, d), lambda i, j: (0, 0))],
        out_specs=pl.BlockSpec((None, tr, d), lambda i, j: (i, j, 0)),
        out_shape=jax.ShapeDtypeStruct((b, l, d), BF16),
        compiler_params=_cparams("arbitrary", "arbitrary"),
        name="adaln",
    )(x, shift, scale, nw.reshape(1, d))


def _mm_kernel(a_ref, w_ref, o_ref):
    o_ref[...] = jnp.dot(a_ref[...], w_ref[...], preferred_element_type=F32).astype(o_ref.dtype)


def _mm_call(a, w, col0, ncols, out_dtype):
    m, k = a.shape
    tm = _tile(m, 1024, BF16_ROWS)
    tn = _tile(ncols, 512, LANES)
    assert col0 % tn == 0
    off = col0 // tn
    return pl.pallas_call(
        _mm_kernel,
        grid=(m // tm, ncols // tn),
        in_specs=[pl.BlockSpec((tm, k), lambda i, j: (i, 0)),
                  pl.BlockSpec((k, tn), lambda i, j: (0, j + off))],
        out_specs=pl.BlockSpec((tm, tn), lambda i, j: (i, j)),
        out_shape=jax.ShapeDtypeStruct((m, ncols), out_dtype),
        compiler_params=_cparams("arbitrary", "arbitrary"),
        name="matmul",
    )(a, w)


def _mm_ug_kernel(a_ref, wu_ref, wg_ref, o_ref, *, tmm):
    a = a_ref[...].reshape(SUBLANES * tmm, a_ref.shape[-1])
    u = jnp.dot(a, wu_ref[...], preferred_element_type=F32).astype(BF16).astype(F32)
    g = jnp.dot(a, wg_ref[...], preferred_element_type=F32).astype(BF16).astype(F32)
    word = lax.bitcast_convert_type(u, jnp.uint32) | (lax.bitcast_convert_type(g, jnp.uint32) >> 16)
    for seg in range(SUBLANES):
        for s in range(o_ref.shape[0]):
            o_ref[s, pl.ds(seg, tmm, stride=SUBLANES), :] = (
                word[seg * tmm:(seg + 1) * tmm, s * LANES:(s + 1) * LANES])


def _mm_ug_call(h, w, r):
    b, l, k = h.shape
    lseg = l // SUBLANES
    tmm = _tile(lseg, 128, BF16_ROWS)
    tn = _tile(r, 256, LANES)
    nslab = tn // LANES
    n_m = lseg // tmm
    return pl.pallas_call(
        functools.partial(_mm_ug_kernel, tmm=tmm),
        grid=(b * n_m, r // tn),
        in_specs=[pl.BlockSpec((None, SUBLANES, tmm, k), lambda i, j: (i // n_m, 0, i % n_m, 0)),
                  pl.BlockSpec((k, tn), lambda i, j: (0, j)),
                  pl.BlockSpec((k, tn), lambda i, j: (0, j + r // tn))],
        out_specs=pl.BlockSpec((None, nslab, SUBLANES * tmm, LANES), lambda i, j: (i // n_m, j, i % n_m, 0)),
        out_shape=jax.ShapeDtypeStruct((b, r // LANES, l, LANES), jnp.uint32),
        compiler_params=_cparams("arbitrary", "arbitrary"),
        name="matmul_ug",
    )(h.reshape(b, SUBLANES, lseg, k), w, w)


def _mm_resid_kernel(a_ref, w_ref, x_ref, g_ref, o_ref, *, row, tiles_per_batch):
    r = (pl.program_id(0) // tiles_per_batch) if row is None else row
    y = jnp.dot(a_ref[...], w_ref[...], preferred_element_type=F32)
    o_ref[...] = x_ref[...] + g_ref[pl.ds(r, 1), :] * y


def _mm_resid_call(a, w, x, gate, rows_per_batch, row):
    m, k = a.shape
    n = w.shape[1]
    tm = _tile(rows_per_batch, 1024, BF16_ROWS)
    tn = _tile(n, 512, LANES)
    return pl.pallas_call(
        functools.partial(_mm_resid_kernel, row=row, tiles_per_batch=rows_per_batch // tm),
        grid=(m // tm, n // tn),
        in_specs=[pl.BlockSpec((tm, k), lambda i, j: (i, 0)),
                  pl.BlockSpec((k, tn), lambda i, j: (0, j)),
                  pl.BlockSpec((tm, tn), lambda i, j: (i, j)),
                  pl.BlockSpec((MOD_ROWS, tn), lambda i, j: (0, j))],
        out_specs=pl.BlockSpec((tm, tn), lambda i, j: (i, j)),
        out_shape=jax.ShapeDtypeStruct((m, n), F32),
        compiler_params=_cparams("arbitrary", "arbitrary"),
        name="matmul_resid",
    )(a, w, x, gate)


def _rglru_kernel(pl_ref, pc_ref, cw_ref, cb_ref, wg_ref, bg_ref, lam_ref, zl_ref, zc_ref,
                  cu_ref, hl_ref, pp_ref, zs_ref, *, s_len, c_len, bw):
    nslab = bw // LANES
    cw = cw_ref[...]
    cb = cb_ref[...]
    nlam = -lam_ref[...]
    softplus = jnp.maximum(nlam, 0.0) + jnp.log1p(jnp.exp(-jnp.abs(nlam)))
    k2 = (-0.5 * LRU_C * LOG2E) * softplus
    sub = lax.broadcasted_iota(jnp.int32, (SUBLANES, bw), 0)

    def unpack(p_ref, r0, n, low):
        parts = []
        for s in range(nslab):
            w = p_ref[s, pl.ds(r0, n), :]
            w = (w << 16) if low else (w & jnp.uint32(0xFFFF0000))
            parts.append(lax.bitcast_convert_type(w, F32))
        return jnp.concatenate(parts, axis=1)

    def seg_down(v):
        return jnp.where(sub == 0, 0.0, pltpu.roll(v, 1, 0))

    def seg_up(v):
        return jnp.where(sub == SUBLANES - 1, 0.0, pltpu.roll(v, SUBLANES - 1, 0))

    def conv_chunk(p_ref, k, n, n_chunks):
        length = n * n_chunks
        r0 = pl.multiple_of(k * n, n)
        lo = CONV_PAD_LO * SUBLANES
        hi = (CONV_W - 1 - CONV_PAD_LO) * SUBLANES
        cur = unpack(p_ref, r0, n, False)
        p0 = pl.multiple_of(jnp.where(k == 0, length - lo, r0 - lo), SUBLANES)
        n0 = pl.multiple_of(jnp.where(k == n_chunks - 1, 0, r0 + n), SUBLANES)
        prev = unpack(p_ref, p0, lo, False)
        nxt = unpack(p_ref, n0, hi, False)
        prev_edge = jnp.concatenate([seg_down(prev[i * SUBLANES:(i + 1) * SUBLANES])
                                     for i in range(CONV_PAD_LO)], axis=0)
        nxt_edge = jnp.concatenate([seg_up(nxt[i * SUBLANES:(i + 1) * SUBLANES])
                                    for i in range(CONV_W - 1 - CONV_PAD_LO)], axis=0)
        prev = jnp.where(k == 0, prev_edge, prev)
        nxt = jnp.where(k == n_chunks - 1, nxt_edge, nxt)
        ext = jnp.concatenate([prev, cur, nxt], axis=0)
        acc = cb + cw[0:1] * ext[0:n]
        for t in range(1, CONV_W):
            acc = acc + cw[t:t + 1] * ext[t * SUBLANES:t * SUBLANES + n]
        return acc

    def coeffs(uc, d):
        lo, hi = 2 * d * bw, 2 * (d + 1) * bw
        zg = jnp.dot(uc.astype(BF16), wg_ref[:, lo:hi], preferred_element_type=F32) + bg_ref[:, lo:hi]
        tr = jnp.tanh(zg[:, :bw])
        ti = jnp.tanh(zg[:, bw:])
        kd = k2[d:d + 1]
        la2 = kd * tr + kd
        a = jnp.exp2(la2)
        th = jnp.tanh(la2 * (-LN2))
        root = jnp.where(th > 0.0, th * lax.rsqrt(th * th + th), 0.0)
        return a, root * ((ti + 1.0) * uc)

    def run_sequence(p_ref, z_ref, length, h0f, h0b):
        lseg = length // SUBLANES
        steps = _tile(lseg, SCAN_STEPS, BF16_ROWS)
        n = steps * SUBLANES
        n_chunks = length // n

        def conv_body(k, carry):
            cu_ref[pl.ds(pl.multiple_of(k * n, n), n), :] = conv_chunk(p_ref, k, n, n_chunks)
            return carry

        lax.fori_loop(0, n_chunks, conv_body, 0)

        def scan_body(k, carry):
            hf, pf, hb, pb = carry
            rf = pl.multiple_of(k * n, n)
            rb = pl.multiple_of((n_chunks - 1 - k) * n, n)
            af, bf = coeffs(cu_ref[pl.ds(rf, n), :], 0)
            ab, bb = coeffs(cu_ref[pl.ds(rb, n), :], 1)
            hs_f, ps_f, hs_b, ps_b = [None] * steps, [None] * steps, [None] * steps, [None] * steps
            for m in range(steps):
                sf = slice(m * SUBLANES, (m + 1) * SUBLANES)
                hf = af[sf] * hf + bf[sf]
                pf = af[sf] * pf
                hs_f[m], ps_f[m] = hf, pf
                mb = steps - 1 - m
                sb = slice(mb * SUBLANES, (mb + 1) * SUBLANES)
                hb = ab[sb] * hb + bb[sb]
                pb = ab[sb] * pb
                hs_b[mb], ps_b[mb] = hb, pb
            hl_ref[0, pl.ds(rf, n), :] = jnp.concatenate(hs_f, axis=0)
            pp_ref[0, pl.ds(rf, n), :] = jnp.concatenate(ps_f, axis=0)
            hl_ref[1, pl.ds(rb, n), :] = jnp.concatenate(hs_b, axis=0)
            pp_ref[1, pl.ds(rb, n), :] = jnp.concatenate(ps_b, axis=0)
            return hf, pf, hb, pb

        zeros = jnp.zeros((SUBLANES, bw), F32)
        ones = jnp.ones((SUBLANES, bw), F32)
        ef, qf, eb, qb = lax.fori_loop(0, n_chunks, scan_body, (zeros, ones, zeros, ones))

        rows_f, rows_b = [None] * SUBLANES, [None] * SUBLANES
        c = h0f
        for s in range(SUBLANES):
            rows_f[s] = c
            c = qf[s:s + 1] * c + ef[s:s + 1]
        hf_end = c
        c = h0b
        for s in range(SUBLANES - 1, -1, -1):
            rows_b[s] = c
            c = qb[s:s + 1] * c + eb[s:s + 1]
        hb_end = c
        cf = jnp.concatenate(rows_f, axis=0)[None]
        cbk = jnp.concatenate(rows_b, axis=0)[None]

        def out_body(k, carry):
            r0 = pl.multiple_of(k * n, n)
            sl = pl.ds(r0, n)
            hf = hl_ref[0, sl, :].reshape(steps, SUBLANES, bw) + pp_ref[0, sl, :].reshape(steps, SUBLANES, bw) * cf
            hb = hl_ref[1, sl, :].reshape(steps, SUBLANES, bw) + pp_ref[1, sl, :].reshape(steps, SUBLANES, bw) * cbk
            g = unpack(p_ref, r0, n, True)
            gate = (g * (0.25 * SQRT2)) * (jnp.tanh(g * 0.5) + 1.0)
            z = (hf + hb).reshape(n, bw) * gate
            for s in range(nslab):
                zs_ref[s, 0:n, :] = z[:, s * LANES:(s + 1) * LANES]
            for seg in range(SUBLANES):
                piece = jnp.concatenate([zs_ref[s, pl.ds(seg, steps, stride=SUBLANES), :]
                                         for s in range(nslab)], axis=1)
                t0 = pl.multiple_of(seg * lseg + k * steps, steps)
                z_ref[pl.ds(t0, steps), :] = piece.astype(z_ref.dtype)
            return carry

        lax.fori_loop(0, n_chunks, out_body, 0)
        return hf_end, hb_end

    zero = jnp.zeros((1, bw), F32)
    hf0, hb0 = run_sequence(pc_ref, zc_ref, c_len, zero, zero)
    run_sequence(pl_ref, zl_ref, s_len, hf0, hb0)


def _rglru_call(p_lat, p_ctx, conv_w, conv_b, w_gate, b_gate, lam):
    b, nsl, s_len, _ = p_lat.shape
    c_len = p_ctx.shape[2]
    r = nsl * LANES
    nblk, bw, _ = w_gate.shape
    nslab = bw // LANES
    lmax = max(s_len, c_len)
    chunk = SUBLANES * SCAN_STEPS
    kern = functools.partial(_rglru_kernel, s_len=s_len, c_len=c_len, bw=bw)
    return pl.pallas_call(
        kern,
        grid=(b, nblk),
        in_specs=[pl.BlockSpec((None, nslab, s_len, LANES), lambda i, j: (i, j, 0, 0)),
                  pl.BlockSpec((None, nslab, c_len, LANES), lambda i, j: (i, j, 0, 0)),
                  pl.BlockSpec((CONV_W, bw), lambda i, j: (0, j)),
                  pl.BlockSpec((1, bw), lambda i, j: (0, j)),
                  pl.BlockSpec((None, bw, 4 * bw), lambda i, j: (j, 0, 0)),
                  pl.BlockSpec((None, 1, 4 * bw), lambda i, j: (j, 0, 0)),
                  pl.BlockSpec((2, bw), lambda i, j: (0, j))],
        out_specs=[pl.BlockSpec((None, s_len, bw), lambda i, j: (i, 0, j)),
                   pl.BlockSpec((None, c_len, bw), lambda i, j: (i, 0, j))],
        out_shape=[jax.ShapeDtypeStruct((b, s_len, r), BF16),
                   jax.ShapeDtypeStruct((b, c_len, r), BF16)],
        scratch_shapes=[pltpu.VMEM((lmax, bw), F32),
                        pltpu.VMEM((2, lmax, bw), F32),
                        pltpu.VMEM((2, lmax, bw), F32),
                        pltpu.VMEM((nslab, chunk, LANES), F32)],
        compiler_params=_cparams("arbitrary", "arbitrary"),
        name="rglru",
    )(p_lat, p_ctx, conv_w, conv_b.reshape(1, r), w_gate, b_gate, lam)


def _qkprep_kernel(x_ref, cos_ref, sin_ref, w_ref, o_ref, *, n_heads, rope):
    w = w_ref[...]
    tr = x_ref.shape[0]
    if rope:
        cos = cos_ref[...]
        sin = sin_ref[...]
        lane = lax.broadcasted_iota(jnp.int32, (tr, HEAD_DIM), 1)
        first_half = (lane % (HEAD_DIM // 2)) < (HEAD_DIM // 4)
    for h in range(n_heads):
        sl = slice(h * HEAD_DIM, (h + 1) * HEAD_DIM)
        x = x_ref[:, sl].astype(F32)
        inv = lax.rsqrt(jnp.mean(x * x, axis=-1, keepdims=True) + NORM_EPS)
        xn = (x * inv) * w
        if rope:
            partner = jnp.where(first_half,
                                pltpu.roll(xn, HEAD_DIM - HEAD_DIM // 4, 1),
                                pltpu.roll(xn, HEAD_DIM // 4, 1))
            xn = xn * cos + partner * sin
        o_ref[:, sl] = xn.astype(o_ref.dtype)


def _qkprep_call(x, col0, width, cos, sin, w, rope):
    b, l, _ = x.shape
    tr = _tile(l, 256, BF16_ROWS)
    assert col0 % width == 0
    off = col0 // width
    return pl.pallas_call(
        functools.partial(_qkprep_kernel, n_heads=width // HEAD_DIM, rope=rope),
        grid=(b, l // tr),
        in_specs=[pl.BlockSpec((None, tr, width), lambda i, j: (i, j, off)),
                  pl.BlockSpec((tr, HEAD_DIM), lambda i, j: (j, 0)),
                  pl.BlockSpec((tr, HEAD_DIM), lambda i, j: (j, 0)),
                  pl.BlockSpec((1, HEAD_DIM), lambda i, j: (0, 0))],
        out_specs=pl.BlockSpec((None, tr, width), lambda i, j: (i, j, 0)),
        out_shape=jax.ShapeDtypeStruct((b, l, width), BF16),
        compiler_params=_cparams("arbitrary", "arbitrary"),
        name="qkprep",
    )(x, cos, sin, w.reshape(1, HEAD_DIM))


def _dot_nt(a, b):
    return lax.dot_general(a, b, (((1,), (1,)), ((), ())), preferred_element_type=F32)


def _attn_kernel(sink_ref, bias_ref, q_ref, kp_ref, km_ref, kn_ref, vp_ref, vm_ref, vn_ref, kc_ref, vc_ref,
                 g_ref, o_ref, *, groups, qblocks, n_steps):
    kvh = pl.program_id(1)
    step = pl.program_id(2)
    blk = ATTN_BLOCK
    kw = jnp.concatenate([kp_ref[...], km_ref[...], kn_ref[...]], axis=0)
    vw = jnp.concatenate([vp_ref[...], vm_ref[...], vn_ref[...]], axis=0)
    kc = kc_ref[...]
    vt_c = jnp.concatenate([vc_ref[...].T, jnp.ones((BF16_ROWS, kc.shape[0]), BF16)], axis=0)
    vt_w = jnp.concatenate([vw.T, jnp.ones((BF16_ROWS, vw.shape[0]), BF16)], axis=0)
    c2 = (HEAD_DIM ** -0.5) * LOG2E
    sink2 = jnp.concatenate([jnp.full((1, blk), sink_ref[kvh * groups + g] * LOG2E, F32)
                             for g in range(groups)], axis=1)
    for i in range(qblocks):
        q = q_ref[i * blk:(i + 1) * blk, :]
        qs = jnp.concatenate([q[:, g * HEAD_DIM:(g + 1) * HEAD_DIM] for g in range(groups)], axis=0)
        first = (step == 0).astype(jnp.int32) if i == 0 else 0
        last = (step == n_steps - 1).astype(jnp.int32) if i == qblocks - 1 else 0
        bias = bias_ref[first + 2 * last]
        s_c = _dot_nt(kc, qs) * c2
        s_w = _dot_nt(kw[i * blk:(i + 3) * blk], qs) * c2 + bias
        m = jnp.maximum(jnp.maximum(jnp.max(s_c, axis=0, keepdims=True),
                                    jnp.max(s_w, axis=0, keepdims=True)), sink2)
        p_c = jnp.exp2(s_c - m).astype(BF16)
        p_w = jnp.exp2(s_w - m).astype(BF16)
        ot = (jnp.dot(vt_c, p_c, preferred_element_type=F32)
              + jnp.dot(vt_w[:, i * blk:(i + 3) * blk], p_w, preferred_element_type=F32))
        den = ot[HEAD_DIM:HEAD_DIM + 1, :] + jnp.exp2(sink2 - m)
        ot = ot[:HEAD_DIM, :] * (1.0 / den)
        o = jnp.concatenate([ot[:, g * blk:(g + 1) * blk].T for g in range(groups)], axis=1)
        gate = _silu(g_ref[i * blk:(i + 1) * blk, :].astype(F32))
        o_ref[i * blk:(i + 1) * blk, :] = (o * gate).astype(o_ref.dtype)


def _band_bias(groups):
    blk = ATTN_BLOCK
    ki = jnp.arange(3 * blk)[:, None]
    qi = jnp.arange(blk)[None, :]
    band = jnp.abs(ki - blk - qi) <= WINDOW
    out = []
    for var in range(4):
        ok = band & ((ki >= blk) | (var % 2 == 0)) & ((ki < 2 * blk) | (var // 2 == 0))
        out.append(jnp.tile(jnp.where(ok, 0.0, NEG_INF).astype(F32), (1, groups)))
    return jnp.stack(out)


def _attn_call(sink, qn, kn, kcn, qkvg, kvc, attn_w, kv_w):
    b, s_len, _ = qn.shape
    c_len = kcn.shape[1]
    n_kv = kv_w // HEAD_DIM
    groups = attn_w // kv_w
    gw = groups * HEAD_DIM
    blk = ATTN_BLOCK
    nb = s_len // blk
    qblocks = 4 if nb % 4 == 0 else 1
    n_steps = nb // qblocks
    v_off = (attn_w + kv_w) // HEAD_DIM
    g_off = (attn_w + 2 * kv_w) // gw
    assert (attn_w + 2 * kv_w) % gw == 0

    def edge_spec(prev, off):
        if prev:
            return pl.BlockSpec((None, blk, HEAD_DIM),
                                lambda i, h, j: (i, jnp.maximum(j * qblocks - 1, 0), h + off))
        return pl.BlockSpec((None, blk, HEAD_DIM),
                            lambda i, h, j: (i, jnp.minimum((j + 1) * qblocks, nb - 1), h + off))

    def main_spec(off):
        return pl.BlockSpec((None, qblocks * blk, HEAD_DIM), lambda i, h, j: (i, j, h + off))

    def tile_spec(off):
        return pl.BlockSpec((None, qblocks * blk, gw), lambda i, h, j: (i, j, h + off))

    return pl.pallas_call(
        functools.partial(_attn_kernel, groups=groups, qblocks=qblocks, n_steps=n_steps),
        grid=(b, n_kv, n_steps),
        in_specs=[pl.BlockSpec(memory_space=pltpu.SMEM),
                  pl.BlockSpec((4, 3 * blk, groups * blk), lambda i, h, j: (0, 0, 0)),
                  tile_spec(0),
                  edge_spec(True, 0), main_spec(0), edge_spec(False, 0),
                  edge_spec(True, v_off), main_spec(v_off), edge_spec(False, v_off),
                  pl.BlockSpec((None, c_len, HEAD_DIM), lambda i, h, j: (i, 0, h)),
                  pl.BlockSpec((None, c_len, HEAD_DIM), lambda i, h, j: (i, 0, h + n_kv)),
                  tile_spec(g_off)],
        out_specs=tile_spec(0),
        out_shape=jax.ShapeDtypeStruct((b, s_len, attn_w), BF16),
        compiler_params=_cparams("arbitrary", "arbitrary", "arbitrary"),
        name="attention",
    )(sink, _band_bias(groups), qn, kn, kn, kn, qkvg, qkvg, qkvg, kcn, kvc, qkvg)


def _rope_tables(s_len):
    t = jnp.arange(s_len, dtype=jnp.int32)
    pos = jnp.stack([t // GRID_W, t % GRID_W], axis=1).astype(F32)
    half = HEAD_DIM // 2
    quarter = half // 2
    inv_freq = ROPE_BASE ** (-jnp.arange(quarter, dtype=F32) * (2.0 / half))
    ang = pos[:, :, None] * inv_freq[None, None, :]
    cos = jnp.concatenate([jnp.cos(ang), jnp.cos(ang)], axis=-1).reshape(s_len, HEAD_DIM)
    sin = jnp.concatenate([-jnp.sin(ang), jnp.sin(ang)], axis=-1).reshape(s_len, HEAD_DIM)
    return cos, sin


def kernel(x, c, ctx, c_ctx, w_mod, b_mod, norm_w, rg_w_in, rg_conv_w, rg_conv_b, rg_w_r, rg_b_r, rg_w_i,
           rg_b_i, rg_lam, rg_w_out, at_w_in, at_q_norm, at_k_norm, at_sink, at_w_out):
    b, s_len, d = x.shape
    c_len = ctx.shape[1]
    depth = w_mod.shape[0]
    assert depth == 2 and b < MOD_ROWS
    ctx_row = b

    c8 = jnp.concatenate([c, c_ctx[None, :], jnp.zeros((MOD_ROWS - b - 1, d), F32)], axis=0)
    mod = _mod_call(c8, w_mod, b_mod)
    shift, scale, gate = mod[:, :, :d], mod[:, :, d:2 * d], mod[:, :, 2 * d:]

    r = rg_w_out.shape[1]
    h = _adaln_call(x, shift[0], scale[0], norm_w[0], None)
    hc = _adaln_call(ctx, shift[0], scale[0], norm_w[0], ctx_row)
    w_in = rg_w_in[0].astype(BF16)
    p_lat = _mm_ug_call(h, w_in, r)
    p_ctx = _mm_ug_call(hc, w_in, r)
    w_gate = (0.5 * jnp.concatenate([rg_w_r[0, 0], rg_w_i[0, 0], rg_w_r[0, 1], rg_w_i[0, 1]], axis=-1)).astype(BF16)
    nblk, bw = rg_w_r.shape[2], rg_w_r.shape[3]
    b_gate = 0.5 * jnp.concatenate([rg_b_r[0, 0].reshape(nblk, 1, bw), rg_b_i[0, 0].reshape(nblk, 1, bw),
                                    rg_b_r[0, 1].reshape(nblk, 1, bw), rg_b_i[0, 1].reshape(nblk, 1, bw)], axis=-1)
    z, zc = _rglru_call(p_lat, p_ctx, rg_conv_w[0], rg_conv_b[0], w_gate, b_gate, rg_lam[0])
    w_out = rg_w_out[0].astype(BF16)
    x = _mm_resid_call(z.reshape(b * s_len, r), w_out, x.reshape(b * s_len, d), gate[0], s_len, None)
    x = x.reshape(b, s_len, d)
    ctx = _mm_resid_call(zc.reshape(b * c_len, r), w_out, ctx.reshape(b * c_len, d), gate[0], c_len, ctx_row)
    ctx = ctx.reshape(b, c_len, d)

    attn_w = at_w_out.shape[1]
    kv_w = (at_w_in.shape[2] - 2 * attn_w) // 2
    h = _adaln_call(x, shift[1], scale[1], norm_w[1], None)
    hc = _adaln_call(ctx, shift[1], scale[1], norm_w[1], ctx_row)
    w_in = at_w_in[0].astype(BF16)
    qkvg = _mm_call(h.reshape(b * s_len, d), w_in, 0, w_in.shape[1], BF16).reshape(b, s_len, -1)
    kvc = _mm_call(hc.reshape(b * c_len, d), w_in, attn_w, 2 * kv_w, BF16).reshape(b, c_len, 2 * kv_w)
    cos, sin = _rope_tables(s_len)
    qn = _qkprep_call(qkvg, 0, attn_w, cos, sin, at_q_norm[0], True)
    kn = _qkprep_call(qkvg, attn_w, kv_w, cos, sin, at_k_norm[0], True)
    kcn = _qkprep_call(kvc, 0, kv_w, cos[:c_len], sin[:c_len], at_k_norm[0], False)
    z = _attn_call(at_sink[0], qn, kn, kcn, qkvg, kvc, attn_w, kv_w)
    x = _mm_resid_call(z.reshape(b * s_len, attn_w), at_w_out[0].astype(BF16), x.reshape(b * s_len, d),
                       gate[1], s_len, None)
    return x.reshape(b, s_len, d)
```

```python
import functools

import jax
import jax.numpy as jnp
from jax import lax
from jax.experimental import pallas as pl
from jax.experimental.pallas import tpu as pltpu

F32 = jnp.float32
BF16 = jnp.bfloat16

HEAD_DIM = 128
WINDOW = 128
ATTN_BLOCK = 128
GRID_W = 64
ROPE_BASE = 10000.0
NORM_EPS = 1e-6
NEG_INF = -1e30
LRU_C = 8.0
CONV_W = 4
CONV_PAD_LO = 2
LOG2E = 1.4426950408889634
LN2 = 0.6931471805599453
SCORE_SCALE = HEAD_DIM ** -0.5 * LOG2E
SQRT2 = 1.4142135623730951

SUBLANES = 8
LANES = 128
BF16_ROWS = 16
MOD_ROWS = 8
VMEM_LIMIT = 56 * 1024 * 1024
SCAN_STEPS = 64
NORM_ROW_BLOCKS = 4
SCORE_LOOKAHEAD = 2


def _cparams(*sem):
    return pltpu.CompilerParams(dimension_semantics=sem, vmem_limit_bytes=VMEM_LIMIT)


def _tile(n, pref, unit):
    if n <= pref:
        return n
    t = (pref // unit) * unit
    while n % t:
        t -= unit
    return t


def _sigmoid(x):
    return 1.0 / (1.0 + jnp.exp(-x))


def _silu(x):
    return x * _sigmoid(x)


def _mod_kernel(c_ref, w_ref, b_ref, o_ref):
    s = _silu(c_ref[...]).astype(BF16)
    o_ref[...] = jnp.dot(s, w_ref[...].astype(BF16), preferred_element_type=F32) + b_ref[...]


def _mod_call(c8, w_mod, b_mod):
    depth, d, n3 = w_mod.shape
    tn = _tile(n3, 512, LANES)
    return pl.pallas_call(
        _mod_kernel,
        grid=(depth, n3 // tn),
        in_specs=[pl.BlockSpec((MOD_ROWS, d), lambda l, j: (0, 0)),
                  pl.BlockSpec((None, d, tn), lambda l, j: (l, 0, j)),
                  pl.BlockSpec((None, 1, tn), lambda l, j: (l, 0, j))],
        out_specs=pl.BlockSpec((None, MOD_ROWS, tn), lambda l, j: (l, 0, j)),
        out_shape=jax.ShapeDtypeStruct((depth, MOD_ROWS, n3), F32),
        compiler_params=_cparams("arbitrary", "arbitrary"),
        name="mod",
    )(c8, w_mod, b_mod.reshape(depth, 1, n3))


def _adaln_kernel(x_ref, sh_ref, sc_ref, nw_ref, o_ref, *, row):
    r = pl.program_id(0) if row is None else row
    x = x_ref[...]
    inv = lax.rsqrt(jnp.mean(x * x, axis=-1, keepdims=True) + NORM_EPS)
    xn = (x * inv) * nw_ref[...]
    o_ref[...] = (xn * (1.0 + sc_ref[pl.ds(r, 1), :]) + sh_ref[pl.ds(r, 1), :]).astype(o_ref.dtype)


def _adaln_call(x, shift, scale, nw, row):
    b, l, d = x.shape
    tr = _tile(l, 256, BF16_ROWS)
    return pl.pallas_call(
        functools.partial(_adaln_kernel, row=row),
        grid=(b, l // tr),
        in_specs=[pl.BlockSpec((None, tr, d), lambda i, j: (i, j, 0)),
                  pl.BlockSpec((MOD_ROWS, d), lambda i, j: (0, 0)),
                  pl.BlockSpec((MOD_ROWS, d), lambda i, j: (0, 0)),
                  pl.BlockSpec((1, d), lambda i, j: (0, 0))],
        out_specs=pl.BlockSpec((None, tr, d), lambda i, j: (i, j, 0)),
        out_shape=jax.ShapeDtypeStruct((b, l, d), BF16),
        compiler_params=_cparams("arbitrary", "arbitrary"),
        name="adaln",
    )(x, shift, scale, nw.reshape(1, d))


def _mm_kernel(a_ref, w_ref, o_ref):
    o_ref[...] = jnp.dot(a_ref[...], w_ref[...], preferred_element_type=F32).astype(o_ref.dtype)


def _mm_call(a, w, col0, ncols, out_dtype):
    m, k = a.shape
    tm = _tile(m, 1024, BF16_ROWS)
    tn = _tile(ncols, 512, LANES)
    assert col0 % tn == 0
    off = col0 // tn
    return pl.pallas_call(
        _mm_kernel,
        grid=(m // tm, ncols // tn),
        in_specs=[pl.BlockSpec((tm, k), lambda i, j: (i, 0)),
                  pl.BlockSpec((k, tn), lambda i, j: (0, j + off))],
        out_specs=pl.BlockSpec((tm, tn), lambda i, j: (i, j)),
        out_shape=jax.ShapeDtypeStruct((m, ncols), out_dtype),
        compiler_params=_cparams("arbitrary", "arbitrary"),
        name="matmul",
    )(a, w)


def _mm_qkvg_kernel(a_ref, w_ref, cos_ref, sin_ref, nw_ref, o_ref, *, n_norm_tiles):
    j = pl.program_id(1)

    @pl.when(j < n_norm_tiles)
    def _():
        tm = a_ref.shape[0]
        rb = tm // NORM_ROW_BLOCKS if tm % (NORM_ROW_BLOCKS * BF16_ROWS) == 0 else tm
        for r0 in range(0, tm, rb):
            y = jnp.dot(a_ref[r0:r0 + rb, :], w_ref[...], preferred_element_type=F32)
            cos = cos_ref[r0:r0 + rb, :]
            sin = sin_ref[r0:r0 + rb, :]
            for h in range(y.shape[1] // HEAD_DIM):
                sl = slice(h * HEAD_DIM, (h + 1) * HEAD_DIM)
                x = y[:, sl]
                inv = lax.rsqrt(jnp.mean(x * x, axis=-1, keepdims=True) + NORM_EPS)
                xn = (x * inv) * nw_ref[:, sl]
                o_ref[r0:r0 + rb, sl] = (xn * cos + pltpu.roll(xn, HEAD_DIM // 2, 1) * sin).astype(o_ref.dtype)

    @pl.when(j >= n_norm_tiles)
    def _():
        o_ref[...] = jnp.dot(a_ref[...], w_ref[...], preferred_element_type=F32).astype(o_ref.dtype)


def _mm_qkvg_call(a, w, cos, sin, nw, s_len):
    m, k = a.shape
    n = w.shape[1]
    tm = _tile(s_len, 1024, BF16_ROWS)
    tn = _tile(nw.shape[1], 512, HEAD_DIM)
    assert n % tn == 0
    n_norm_tiles = nw.shape[1] // tn
    t_per_seq = s_len // tm
    return pl.pallas_call(
        functools.partial(_mm_qkvg_kernel, n_norm_tiles=n_norm_tiles),
        grid=(m // tm, n // tn),
        in_specs=[pl.BlockSpec((tm, k), lambda i, j: (i, 0)),
                  pl.BlockSpec((k, tn), lambda i, j: (0, j)),
                  pl.BlockSpec((tm, HEAD_DIM), lambda i, j: (i % t_per_seq, 0)),
                  pl.BlockSpec((tm, HEAD_DIM), lambda i, j: (i % t_per_seq, 0)),
                  pl.BlockSpec((1, tn), lambda i, j: (0, jnp.minimum(j, n_norm_tiles - 1)))],
        out_specs=pl.BlockSpec((tm, tn), lambda i, j: (i, j)),
        out_shape=jax.ShapeDtypeStruct((m, n), BF16),
        compiler_params=_cparams("arbitrary", "arbitrary"),
        name="matmul_qkvg",
    )(a, w, cos, sin, nw)


def _mm_ug_kernel(a_ref, wu_ref, wg_ref, o_ref, *, tmm):
    a = a_ref[...].reshape(SUBLANES * tmm, a_ref.shape[-1])
    u = jnp.dot(a, wu_ref[...], preferred_element_type=F32).astype(BF16).astype(F32)
    g = jnp.dot(a, wg_ref[...], preferred_element_type=F32).astype(BF16).astype(F32)
    word = lax.bitcast_convert_type(u, jnp.uint32) | (lax.bitcast_convert_type(g, jnp.uint32) >> 16)
    for seg in range(SUBLANES):
        for s in range(o_ref.shape[0]):
            o_ref[s, pl.ds(seg, tmm, stride=SUBLANES), :] = (
                word[seg * tmm:(seg + 1) * tmm, s * LANES:(s + 1) * LANES])


def _mm_ug_call(h, w, r):
    b, l, k = h.shape
    lseg = l // SUBLANES
    tmm = _tile(lseg, 128, BF16_ROWS)
    tn = _tile(r, 256, LANES)
    nslab = tn // LANES
    n_m = lseg // tmm
    return pl.pallas_call(
        functools.partial(_mm_ug_kernel, tmm=tmm),
        grid=(b * n_m, r // tn),
        in_specs=[pl.BlockSpec((None, SUBLANES, tmm, k), lambda i, j: (i // n_m, 0, i % n_m, 0)),
                  pl.BlockSpec((k, tn), lambda i, j: (0, j)),
                  pl.BlockSpec((k, tn), lambda i, j: (0, j + r // tn))],
        out_specs=pl.BlockSpec((None, nslab, SUBLANES * tmm, LANES), lambda i, j: (i // n_m, j, i % n_m, 0)),
        out_shape=jax.ShapeDtypeStruct((b, r // LANES, l, LANES), jnp.uint32),
        compiler_params=_cparams("arbitrary", "arbitrary"),
        name="matmul_ug",
    )(h.reshape(b, SUBLANES, lseg, k), w, w)


def _mm_resid_kernel(a_ref, w_ref, x_ref, g_ref, o_ref, *, row, tiles_per_batch):
    r = (pl.program_id(0) // tiles_per_batch) if row is None else row
    y = jnp.dot(a_ref[...], w_ref[...], preferred_element_type=F32)
    o_ref[...] = x_ref[...] + g_ref[pl.ds(r, 1), :] * y


def _mm_resid_call(a, w, x, gate, rows_per_batch, row):
    m, k = a.shape
    n = w.shape[1]
    tm = _tile(rows_per_batch, 1024, BF16_ROWS)
    tn = _tile(n, 512, LANES)
    return pl.pallas_call(
        functools.partial(_mm_resid_kernel, row=row, tiles_per_batch=rows_per_batch // tm),
        grid=(m // tm, n // tn),
        in_specs=[pl.BlockSpec((tm, k), lambda i, j: (i, 0)),
                  pl.BlockSpec((k, tn), lambda i, j: (0, j)),
                  pl.BlockSpec((tm, tn), lambda i, j: (i, j)),
                  pl.BlockSpec((MOD_ROWS, tn), lambda i, j: (0, j))],
        out_specs=pl.BlockSpec((tm, tn), lambda i, j: (i, j)),
        out_shape=jax.ShapeDtypeStruct((m, n), F32),
        compiler_params=_cparams("arbitrary", "arbitrary"),
        name="matmul_resid",
    )(a, w, x, gate)


def _rglru_kernel(pl_ref, pc_ref, cw_ref, cb_ref, wg_ref, bg_ref, lam_ref, zl_ref, zc_ref,
                  cu_ref, hl_ref, pp_ref, zs_ref, *, s_len, c_len, bw):
    nslab = bw // LANES
    cw = cw_ref[...]
    cb = cb_ref[...]
    nlam = -lam_ref[...]
    softplus = jnp.maximum(nlam, 0.0) + jnp.log1p(jnp.exp(-jnp.abs(nlam)))
    k2 = (-0.5 * LRU_C * LOG2E) * softplus
    sub = lax.broadcasted_iota(jnp.int32, (SUBLANES, bw), 0)

    def unpack(p_ref, r0, n, low):
        parts = []
        for s in range(nslab):
            w = p_ref[s, pl.ds(r0, n), :]
            w = (w << 16) if low else (w & jnp.uint32(0xFFFF0000))
            parts.append(lax.bitcast_convert_type(w, F32))
        return jnp.concatenate(parts, axis=1)

    def seg_down(v):
        return jnp.where(sub == 0, 0.0, pltpu.roll(v, 1, 0))

    def seg_up(v):
        return jnp.where(sub == SUBLANES - 1, 0.0, pltpu.roll(v, SUBLANES - 1, 0))

    def conv_chunk(p_ref, k, n, n_chunks):
        length = n * n_chunks
        r0 = pl.multiple_of(k * n, n)
        lo = CONV_PAD_LO * SUBLANES
        hi = (CONV_W - 1 - CONV_PAD_LO) * SUBLANES
        cur = unpack(p_ref, r0, n, False)
        p0 = pl.multiple_of(jnp.where(k == 0, length - lo, r0 - lo), SUBLANES)
        n0 = pl.multiple_of(jnp.where(k == n_chunks - 1, 0, r0 + n), SUBLANES)
        prev = unpack(p_ref, p0, lo, False)
        nxt = unpack(p_ref, n0, hi, False)
        prev_edge = jnp.concatenate([seg_down(prev[i * SUBLANES:(i + 1) * SUBLANES])
                                     for i in range(CONV_PAD_LO)], axis=0)
        nxt_edge = jnp.concatenate([seg_up(nxt[i * SUBLANES:(i + 1) * SUBLANES])
                                    for i in range(CONV_W - 1 - CONV_PAD_LO)], axis=0)
        prev = jnp.where(k == 0, prev_edge, prev)
        nxt = jnp.where(k == n_chunks - 1, nxt_edge, nxt)
        ext = jnp.concatenate([prev, cur, nxt], axis=0)
        acc = cb + cw[0:1] * ext[0:n]
        for t in range(1, CONV_W):
            acc = acc + cw[t:t + 1] * ext[t * SUBLANES:t * SUBLANES + n]
        return acc

    def coeffs(uc, d):
        lo, hi = 2 * d * bw, 2 * (d + 1) * bw
        zg = jnp.dot(uc.astype(BF16), wg_ref[:, lo:hi], preferred_element_type=F32) + bg_ref[:, lo:hi]
        tr = jnp.tanh(zg[:, :bw])
        ti = jnp.tanh(zg[:, bw:])
        kd = k2[d:d + 1]
        la2 = kd * tr + kd
        a = jnp.exp2(la2)
        th = jnp.tanh(la2 * (-LN2))
        root = jnp.where(th > 0.0, th * lax.rsqrt(th * th + th), 0.0)
        return a, root * ((ti + 1.0) * uc)

    def run_sequence(p_ref, z_ref, length, h0f, h0b):
        lseg = length // SUBLANES
        steps = _tile(lseg, SCAN_STEPS, BF16_ROWS)
        n = steps * SUBLANES
        n_chunks = length // n

        def conv_body(k, carry):
            cu_ref[pl.ds(pl.multiple_of(k * n, n), n), :] = conv_chunk(p_ref, k, n, n_chunks)
            return carry

        lax.fori_loop(0, n_chunks, conv_body, 0)

        def chunk_coeffs(k):
            return (coeffs(cu_ref[pl.ds(k * n, n), :], 0)
                    + coeffs(cu_ref[pl.ds((n_chunks - 1 - k) * n, n), :], 1))

        def chunk_scan(k, co, carry):
            af, bf, ab, bb = co
            hf, pf, hb, pb = carry
            rf = k * n
            rb = (n_chunks - 1 - k) * n
            hs_f, ps_f, hs_b, ps_b = [None] * steps, [None] * steps, [None] * steps, [None] * steps
            for m in range(steps):
                sf = slice(m * SUBLANES, (m + 1) * SUBLANES)
                hf = af[sf] * hf + bf[sf]
                pf = af[sf] * pf
                hs_f[m], ps_f[m] = hf, pf
                mb = steps - 1 - m
                sb = slice(mb * SUBLANES, (mb + 1) * SUBLANES)
                hb = ab[sb] * hb + bb[sb]
                pb = ab[sb] * pb
                hs_b[mb], ps_b[mb] = hb, pb
            hl_ref[0, pl.ds(rf, n), :] = jnp.concatenate(hs_f, axis=0)
            pp_ref[0, pl.ds(rf, n), :] = jnp.concatenate(ps_f, axis=0)
            hl_ref[1, pl.ds(rb, n), :] = jnp.concatenate(hs_b, axis=0)
            pp_ref[1, pl.ds(rb, n), :] = jnp.concatenate(ps_b, axis=0)
            return hf, pf, hb, pb

        carry = (jnp.zeros((SUBLANES, bw), F32), jnp.ones((SUBLANES, bw), F32),
                 jnp.zeros((SUBLANES, bw), F32), jnp.ones((SUBLANES, bw), F32))
        nxt = chunk_coeffs(0)
        for k in range(n_chunks):
            cur = nxt
            if k + 1 < n_chunks:
                nxt = chunk_coeffs(k + 1)
            carry = chunk_scan(k, cur, carry)
        ef, qf, eb, qb = carry

        rows_f, rows_b = [None] * SUBLANES, [None] * SUBLANES
        c = h0f
        for s in range(SUBLANES):
            rows_f[s] = c
            c = qf[s:s + 1] * c + ef[s:s + 1]
        hf_end = c
        c = h0b
        for s in range(SUBLANES - 1, -1, -1):
            rows_b[s] = c
            c = qb[s:s + 1] * c + eb[s:s + 1]
        hb_end = c
        cf = jnp.concatenate(rows_f, axis=0)[None]
        cbk = jnp.concatenate(rows_b, axis=0)[None]

        def out_body(k, carry):
            r0 = pl.multiple_of(k * n, n)
            sl = pl.ds(r0, n)
            hf = hl_ref[0, sl, :].reshape(steps, SUBLANES, bw) + pp_ref[0, sl, :].reshape(steps, SUBLANES, bw) * cf
            hb = hl_ref[1, sl, :].reshape(steps, SUBLANES, bw) + pp_ref[1, sl, :].reshape(steps, SUBLANES, bw) * cbk
            g = unpack(p_ref, r0, n, True)
            gate = (g * (0.25 * SQRT2)) * (jnp.tanh(g * 0.5) + 1.0)
            z = (hf + hb).reshape(n, bw) * gate
            for s in range(nslab):
                zs_ref[s, 0:n, :] = z[:, s * LANES:(s + 1) * LANES]
            for seg in range(SUBLANES):
                piece = jnp.concatenate([zs_ref[s, pl.ds(seg, steps, stride=SUBLANES), :]
                                         for s in range(nslab)], axis=1)
                t0 = pl.multiple_of(seg * lseg + k * steps, steps)
                z_ref[pl.ds(t0, steps), :] = piece.astype(z_ref.dtype)
            return carry

        lax.fori_loop(0, n_chunks, out_body, 0)
        return hf_end, hb_end

    zero = jnp.zeros((1, bw), F32)
    hf0, hb0 = run_sequence(pc_ref, zc_ref, c_len, zero, zero)
    run_sequence(pl_ref, zl_ref, s_len, hf0, hb0)


def _rglru_call(p_lat, p_ctx, conv_w, conv_b, w_gate, b_gate, lam):
    b, nsl, s_len, _ = p_lat.shape
    c_len = p_ctx.shape[2]
    r = nsl * LANES
    nblk, bw, _ = w_gate.shape
    nslab = bw // LANES
    lmax = max(s_len, c_len)
    chunk = SUBLANES * SCAN_STEPS
    kern = functools.partial(_rglru_kernel, s_len=s_len, c_len=c_len, bw=bw)
    return pl.pallas_call(
        kern,
        grid=(b, nblk),
        in_specs=[pl.BlockSpec((None, nslab, s_len, LANES), lambda i, j: (i, j, 0, 0)),
                  pl.BlockSpec((None, nslab, c_len, LANES), lambda i, j: (i, j, 0, 0)),
                  pl.BlockSpec((CONV_W, bw), lambda i, j: (0, j)),
                  pl.BlockSpec((1, bw), lambda i, j: (0, j)),
                  pl.BlockSpec((None, bw, 4 * bw), lambda i, j: (j, 0, 0)),
                  pl.BlockSpec((None, 1, 4 * bw), lambda i, j: (j, 0, 0)),
                  pl.BlockSpec((2, bw), lambda i, j: (0, j))],
        out_specs=[pl.BlockSpec((None, s_len, bw), lambda i, j: (i, 0, j)),
                   pl.BlockSpec((None, c_len, bw), lambda i, j: (i, 0, j))],
        out_shape=[jax.ShapeDtypeStruct((b, s_len, r), BF16),
                   jax.ShapeDtypeStruct((b, c_len, r), BF16)],
        scratch_shapes=[pltpu.VMEM((lmax, bw), F32),
                        pltpu.VMEM((2, lmax, bw), F32),
                        pltpu.VMEM((2, lmax, bw), F32),
                        pltpu.VMEM((nslab, chunk, LANES), F32)],
        compiler_params=_cparams("arbitrary", "arbitrary"),
        name="rglru",
    )(p_lat, p_ctx, conv_w, conv_b.reshape(1, r), w_gate, b_gate, lam)


def _headnorm_kernel(x_ref, w_ref, o_ref, *, n_heads):
    w = w_ref[...]
    for h in range(n_heads):
        sl = slice(h * HEAD_DIM, (h + 1) * HEAD_DIM)
        x = x_ref[:, sl].astype(F32)
        inv = lax.rsqrt(jnp.mean(x * x, axis=-1, keepdims=True) + NORM_EPS)
        o_ref[:, sl] = ((x * inv) * w).astype(o_ref.dtype)


def _qkprep_call(x, col0, width, w):
    b, l, _ = x.shape
    tr = _tile(l, 256, BF16_ROWS)
    assert col0 % width == 0
    off = col0 // width
    return pl.pallas_call(
        functools.partial(_headnorm_kernel, n_heads=width // HEAD_DIM),
        grid=(b, l // tr),
        in_specs=[pl.BlockSpec((None, tr, width), lambda i, j: (i, j, off)),
                  pl.BlockSpec((1, HEAD_DIM), lambda i, j: (0, 0))],
        out_specs=pl.BlockSpec((None, tr, width), lambda i, j: (i, j, 0)),
        out_shape=jax.ShapeDtypeStruct((b, l, width), BF16),
        compiler_params=_cparams("arbitrary", "arbitrary"),
        name="ctx_key_norm",
    )(x, w.reshape(1, HEAD_DIM))


def _dot_nt(a, b):
    return lax.dot_general(a, b, (((1,), (1,)), ((), ())), preferred_element_type=F32)


def _attn_kernel(sink_ref, bias_ref, q_ref, kp_ref, km_ref, kn_ref, vp_ref, vm_ref, vn_ref, kc_ref, vc_ref,
                 g_ref, o_ref, *, groups, qblocks, n_steps):
    kvh = pl.program_id(1)
    step = pl.program_id(2)
    blk = ATTN_BLOCK
    kw = jnp.concatenate([kp_ref[...], km_ref[...], kn_ref[...]], axis=0)
    vw = jnp.concatenate([vp_ref[...], vm_ref[...], vn_ref[...]], axis=0)
    kc = kc_ref[...]
    vt_c = jnp.concatenate([vc_ref[...].T, jnp.ones((BF16_ROWS, kc.shape[0]), BF16)], axis=0)
    vt_w = jnp.concatenate([vw.T, jnp.ones((BF16_ROWS, vw.shape[0]), BF16)], axis=0)
    sink2 =jnp.concatenate([jnp.full((1, blk), sink_ref[kvh * groups + g] * LOG2E, F32)
                             for g in range(groups)], axis=1)
    def scores(i):
        q = q_ref[i * blk:(i + 1) * blk, :]
        qs = jnp.concatenate([q[:, g * HEAD_DIM:(g + 1) * HEAD_DIM] for g in range(groups)], axis=0)
        first = (step == 0).astype(jnp.int32) if i == 0 else 0
        last = (step == n_steps - 1).astype(jnp.int32) if i == qblocks - 1 else 0
        bias = bias_ref[first + 2 * last]
        return _dot_nt(kc, qs), _dot_nt(kw[i * blk:(i + 3) * blk], qs) + bias

    ahead = [scores(i) for i in range(min(SCORE_LOOKAHEAD, qblocks))]
    for i in range(qblocks):
        s_c, s_w = ahead.pop(0)
        if i + SCORE_LOOKAHEAD < qblocks:
            ahead.append(scores(i + SCORE_LOOKAHEAD))
        m = jnp.maximum(jnp.maximum(jnp.max(s_c, axis=0, keepdims=True),
                                    jnp.max(s_w, axis=0, keepdims=True)), sink2)
        p_c = jnp.exp2(s_c - m).astype(BF16)
        p_w = jnp.exp2(s_w - m).astype(BF16)
        ot = (jnp.dot(vt_c, p_c, preferred_element_type=F32)
              + jnp.dot(vt_w[:, i * blk:(i + 3) * blk], p_w, preferred_element_type=F32))
        den = ot[HEAD_DIM:HEAD_DIM + 1, :] + jnp.exp2(sink2 - m)
        ot = ot[:HEAD_DIM, :] * (1.0 / den)
        o = jnp.concatenate([ot[:, g * blk:(g + 1) * blk].T for g in range(groups)], axis=1)
        gate = _silu(g_ref[i * blk:(i + 1) * blk, :].astype(F32))
        o_ref[i * blk:(i + 1) * blk, :] = (o * gate).astype(o_ref.dtype)


def _band_bias(groups):
    blk = ATTN_BLOCK
    ki = jnp.arange(3 * blk)[:, None]
    qi = jnp.arange(blk)[None, :]
    band = jnp.abs(ki - blk - qi) <= WINDOW
    out = []
    for var in range(4):
        ok = band & ((ki >= blk) | (var % 2 == 0)) & ((ki < 2 * blk) | (var // 2 == 0))
        out.append(jnp.tile(jnp.where(ok, 0.0, NEG_INF).astype(F32), (1, groups)))
    return jnp.stack(out)


def _attn_call(sink, kcn, qkvg, kvc, attn_w, kv_w):
    b, s_len, _ = qkvg.shape
    c_len = kcn.shape[1]
    n_kv = kv_w // HEAD_DIM
    groups = attn_w // kv_w
    gw = groups * HEAD_DIM
    blk = ATTN_BLOCK
    nb = s_len // blk
    qblocks = next(q for q in (8, 4, 2, 1) if nb % q == 0)
    n_steps = nb // qblocks
    k_off = attn_w // HEAD_DIM
    v_off = (attn_w + kv_w) // HEAD_DIM
    g_off = (attn_w + 2 * kv_w) // gw
    assert (attn_w + 2 * kv_w) % gw == 0

    def edge_spec(prev, off):
        if prev:
            return pl.BlockSpec((None, blk, HEAD_DIM),
                                lambda i, h, j: (i, jnp.maximum(j * qblocks - 1, 0), h + off))
        return pl.BlockSpec((None, blk, HEAD_DIM),
                            lambda i, h, j: (i, jnp.minimum((j + 1) * qblocks, nb - 1), h + off))

    def main_spec(off):
        return pl.BlockSpec((None, qblocks * blk, HEAD_DIM), lambda i, h, j: (i, j, h + off))

    def tile_spec(off):
        return pl.BlockSpec((None, qblocks * blk, gw), lambda i, h, j: (i, j, h + off))

    return pl.pallas_call(
        functools.partial(_attn_kernel, groups=groups, qblocks=qblocks, n_steps=n_steps),
        grid=(b, n_kv, n_steps),
        in_specs=[pl.BlockSpec(memory_space=pltpu.SMEM),
                  pl.BlockSpec((4, 3 * blk, groups * blk), lambda i, h, j: (0, 0, 0)),
                  tile_spec(0),
                  edge_spec(True, k_off), main_spec(k_off), edge_spec(False, k_off),
                  edge_spec(True, v_off), main_spec(v_off), edge_spec(False, v_off),
                  pl.BlockSpec((None, c_len, HEAD_DIM), lambda i, h, j: (i, 0, h)),
                  pl.BlockSpec((None, c_len, HEAD_DIM), lambda i, h, j: (i, 0, h + n_kv)),
                  tile_spec(g_off)],
        out_specs=tile_spec(0),
        out_shape=jax.ShapeDtypeStruct((b, s_len, attn_w), BF16),
        compiler_params=_cparams("arbitrary", "arbitrary", "arbitrary"),
        name="attention",
    )(sink, _band_bias(groups), qkvg, qkvg, qkvg, qkvg, qkvg, qkvg, qkvg, kcn, kvc, qkvg)


def _rope_tables(s_len):
    t = jnp.arange(s_len, dtype=jnp.int32)
    pos = jnp.stack([t // GRID_W, t % GRID_W], axis=1).astype(F32)
    half = HEAD_DIM // 2
    quarter = half // 2
    inv_freq = ROPE_BASE ** (-jnp.arange(quarter, dtype=F32) * (2.0 / half))
    ang = (pos[:, :, None] * inv_freq[None, None, :]).reshape(s_len, half)
    cos = jnp.concatenate([jnp.cos(ang), jnp.cos(ang)], axis=-1)
    sin = jnp.concatenate([-jnp.sin(ang), jnp.sin(ang)], axis=-1)
    return cos, sin


def _permute_heads(t, n_heads):
    quarter = HEAD_DIM // 4
    lead = t.shape[:-1]
    t = t.reshape(lead + (n_heads, 2, 2, quarter))
    t = jnp.swapaxes(t, -2, -3)
    return t.reshape(lead + (n_heads * HEAD_DIM,))


def kernel(x, c, ctx, c_ctx, w_mod, b_mod, norm_w, rg_w_in, rg_conv_w, rg_conv_b, rg_w_r, rg_b_r, rg_w_i,
           rg_b_i, rg_lam, rg_w_out, at_w_in, at_q_norm, at_k_norm, at_sink, at_w_out):
    b, s_len, d = x.shape
    c_len = ctx.shape[1]
    depth = w_mod.shape[0]
    assert depth == 2 and b < MOD_ROWS
    ctx_row = b

    c8 = jnp.concatenate([c, c_ctx[None, :], jnp.zeros((MOD_ROWS - b - 1, d), F32)], axis=0)
    mod = _mod_call(c8, w_mod, b_mod)
    shift, scale, gate = mod[:, :, :d], mod[:, :, d:2 * d], mod[:, :, 2 * d:]

    r = rg_w_out.shape[1]
    h = _adaln_call(x, shift[0], scale[0], norm_w[0], None)
    hc = _adaln_call(ctx, shift[0], scale[0], norm_w[0], ctx_row)
    w_in = rg_w_in[0].astype(BF16)
    p_lat = _mm_ug_call(h, w_in, r)
    p_ctx = _mm_ug_call(hc, w_in, r)
    w_gate = (0.5 * jnp.concatenate([rg_w_r[0, 0], rg_w_i[0, 0], rg_w_r[0, 1], rg_w_i[0, 1]], axis=-1)).astype(BF16)
    nblk, bw = rg_w_r.shape[2], rg_w_r.shape[3]
    b_gate = 0.5 * jnp.concatenate([rg_b_r[0, 0].reshape(nblk, 1, bw), rg_b_i[0, 0].reshape(nblk, 1, bw),
                                    rg_b_r[0, 1].reshape(nblk, 1, bw), rg_b_i[0, 1].reshape(nblk, 1, bw)], axis=-1)
    z, zc = _rglru_call(p_lat, p_ctx, rg_conv_w[0], rg_conv_b[0], w_gate, b_gate, rg_lam[0])
    w_out = rg_w_out[0].astype(BF16)
    x = _mm_resid_call(z.reshape(b * s_len, r), w_out, x.reshape(b * s_len, d), gate[0], s_len, None)
    x = x.reshape(b, s_len, d)
    ctx = _mm_resid_call(zc.reshape(b * c_len, r), w_out, ctx.reshape(b * c_len, d), gate[0], c_len, ctx_row)
    ctx = ctx.reshape(b, c_len, d)

    attn_w = at_w_out.shape[1]
    kv_w = (at_w_in.shape[2] - 2 * attn_w) // 2
    h = _adaln_call(x, shift[1], scale[1], norm_w[1], None)
    hc = _adaln_call(ctx, shift[1], scale[1], norm_w[1], ctx_row)
    n_heads, n_kv = attn_w // HEAD_DIM, kv_w // HEAD_DIM
    w_in = jnp.concatenate([_permute_heads(at_w_in[0, :, :attn_w], n_heads),
                            _permute_heads(at_w_in[0, :, attn_w:attn_w + kv_w], n_kv),
                            at_w_in[0, :, attn_w + kv_w:]], axis=1).astype(BF16)
    q_norm = _permute_heads(at_q_norm[0], 1)
    k_norm = _permute_heads(at_k_norm[0], 1)
    nw = jnp.concatenate([jnp.tile(q_norm * SCORE_SCALE, n_heads), jnp.tile(k_norm, n_kv)])
    nw = nw.reshape(1, attn_w + kv_w)
    cos, sin = _rope_tables(s_len)
    qkvg = _mm_qkvg_call(h.reshape(b * s_len, d), w_in, cos, sin, nw, s_len).reshape(b, s_len, -1)
    kvc = _mm_call(hc.reshape(b * c_len, d), w_in, attn_w, 2 * kv_w, BF16).reshape(b, c_len, 2 * kv_w)
    kcn = _qkprep_call(kvc, 0, kv_w, k_norm)
    z = _attn_call(at_sink[0], kcn, qkvg, kvc, attn_w, kv_w)
    x = _mm_resid_call(z.reshape(b * s_len, attn_w), at_w_out[0].astype(BF16), x.reshape(b * s_len, d),
                       gate[1], s_len, None)
    return x.reshape(b, s_len, d)
```

```python
import functools

import jax
import jax.numpy as jnp
from jax import lax
from jax.experimental import pallas as pl
from jax.experimental.pallas import tpu as pltpu

F32 = jnp.float32
BF16 = jnp.bfloat16

HEAD_DIM = 128
WINDOW = 128
ATTN_BLOCK = 128
GRID_W = 64
ROPE_BASE = 10000.0
NORM_EPS = 1e-6
NEG_INF = -1e30
LRU_C = 8.0
CONV_W = 4
CONV_PAD_LO = 2
LOG2E = 1.4426950408889634
LN2 = 0.6931471805599453
SCORE_SCALE = HEAD_DIM ** -0.5 * LOG2E
SQRT2 = 1.4142135623730951
F32_TINY = 1.1754943508222875e-38

SUBLANES = 8
LANES = 128
BF16_ROWS = 16
MOD_ROWS = 8
VMEM_LIMIT = 56 * 1024 * 1024
SCAN_STEPS = 64
NORM_ROW_BLOCKS = 4
SCORE_LOOKAHEAD = 2


def _cparams(*sem):
    return pltpu.CompilerParams(dimension_semantics=sem, vmem_limit_bytes=VMEM_LIMIT)


def _tile(n, pref, unit):
    if n <= pref:
        return n
    t = (pref // unit) * unit
    while n % t:
        t -= unit
    return t


def _sigmoid(x):
    return 1.0 / (1.0 + jnp.exp(-x))


def _silu(x):
    return x * _sigmoid(x)


def _mod_kernel(c_ref, w_ref, b_ref, o_ref):
    s = _silu(c_ref[...]).astype(BF16)
    o_ref[...] = jnp.dot(s, w_ref[...].astype(BF16), preferred_element_type=F32) + b_ref[...]


def _mod_call(c8, w_mod, b_mod):
    depth, d, n3 = w_mod.shape
    tn = _tile(n3, 768, LANES)
    return pl.pallas_call(
        _mod_kernel,
        grid=(depth, n3 // tn),
        in_specs=[pl.BlockSpec((MOD_ROWS, d), lambda l, j: (0, 0)),
                  pl.BlockSpec((None, d, tn), lambda l, j: (l, 0, j)),
                  pl.BlockSpec((None, 1, tn), lambda l, j: (l, 0, j))],
        out_specs=pl.BlockSpec((None, MOD_ROWS, tn), lambda l, j: (l, 0, j)),
        out_shape=jax.ShapeDtypeStruct((depth, MOD_ROWS, n3), F32),
        compiler_params=_cparams("arbitrary", "arbitrary"),
        name="mod",
    )(c8, w_mod, b_mod.reshape(depth, 1, n3))


def _adaln_kernel(x_ref, sh_ref, sc_ref, nw_ref, o_ref, *, row):
    r = pl.program_id(0) if row is None else row
    x = x_ref[...]
    inv = lax.rsqrt(jnp.mean(x * x, axis=-1, keepdims=True) + NORM_EPS)
    xn = (x * inv) * nw_ref[...]
    o_ref[...] = (xn * (1.0 + sc_ref[pl.ds(r, 1), :]) + sh_ref[pl.ds(r, 1), :]).astype(o_ref.dtype)


def _adaln_call(x, shift, scale, nw, row):
    b, l, d = x.shape
    tr = _tile(l, 512, BF16_ROWS)
    return pl.pallas_call(
        functools.partial(_adaln_kernel, row=row),
        grid=(b, l // tr),
        in_specs=[pl.BlockSpec((None, tr, d), lambda i, j: (i, j, 0)),
                  pl.BlockSpec((MOD_ROWS, d), lambda i, j: (0, 0)),
                  pl.BlockSpec((MOD_ROWS, d), lambda i, j: (0, 0)),
                  pl.BlockSpec((1, d), lambda i, j: (0, 0))],
        out_specs=pl.BlockSpec((None, tr, d), lambda i, j: (i, j, 0)),
        out_shape=jax.ShapeDtypeStruct((b, l, d), BF16),
        compiler_params=_cparams("arbitrary", "arbitrary"),
        name="adaln",
    )(x, shift, scale, nw.reshape(1, d))


def _mm_kernel(a_ref, w_ref, o_ref):
    o_ref[...] = jnp.dot(a_ref[...], w_ref[...], preferred_element_type=F32).astype(o_ref.dtype)


def _mm_call(a, w, col0, ncols, out_dtype):
    m, k = a.shape
    tm = _tile(m, 1024, BF16_ROWS)
    tn = _tile(ncols, 512, LANES)
    assert col0 % tn == 0
    off = col0 // tn
    return pl.pallas_call(
        _mm_kernel,
        grid=(m // tm, ncols // tn),
        in_specs=[pl.BlockSpec((tm, k), lambda i, j: (i, 0)),
                  pl.BlockSpec((k, tn), lambda i, j: (0, j + off))],
        out_specs=pl.BlockSpec((tm, tn), lambda i, j: (i, j)),
        out_shape=jax.ShapeDtypeStruct((m, ncols), out_dtype),
        compiler_params=_cparams("arbitrary", "arbitrary"),
        name="matmul",
    )(a, w)


def _mm_qkvg_kernel(a_ref, w_ref, cos_ref, sin_ref, nw_ref, o_ref, *, n_norm_tiles):
    j = pl.program_id(1)
    quarter = HEAD_DIM // 4

    @pl.when(j < n_norm_tiles)
    def _():
        tm = a_ref.shape[0]
        rb = tm // NORM_ROW_BLOCKS if tm % (NORM_ROW_BLOCKS * BF16_ROWS) == 0 else tm
        for r0 in range(0, tm, rb):
            y = jnp.dot(a_ref[r0:r0 + rb, :], w_ref[...], preferred_element_type=F32)
            cos = cos_ref[r0:r0 + rb, :]
            sin = sin_ref[r0:r0 + rb, :]
            lane = lax.broadcasted_iota(jnp.int32, (rb, HEAD_DIM), 1)
            x1_lanes = (lane % (2 * quarter)) < quarter
            for h in range(y.shape[1] // HEAD_DIM):
                sl = slice(h * HEAD_DIM, (h + 1) * HEAD_DIM)
                x = y[:, sl]
                inv = lax.rsqrt(jnp.mean(x * x, axis=-1, keepdims=True) + NORM_EPS)
                xn = (x * inv) * nw_ref[:, sl]
                partner = jnp.where(x1_lanes, pltpu.roll(xn, HEAD_DIM - quarter, 1), pltpu.roll(xn, quarter, 1))
                o_ref[r0:r0 + rb, sl] = (xn * cos + partner * sin).astype(o_ref.dtype)

    @pl.when(j >= n_norm_tiles)
    def _():
        o_ref[...] = jnp.dot(a_ref[...], w_ref[...], preferred_element_type=F32).astype(o_ref.dtype)


def _mm_qkvg_call(a, w, cos, sin, nw, s_len):
    m, k = a.shape
    n = w.shape[1]
    tm = _tile(s_len, 1024, BF16_ROWS)
    tn = _tile(nw.shape[1], 512, HEAD_DIM)
    assert n % tn == 0
    n_norm_tiles = nw.shape[1] // tn
    t_per_seq = s_len // tm
    return pl.pallas_call(
        functools.partial(_mm_qkvg_kernel, n_norm_tiles=n_norm_tiles),
        grid=(m // tm, n // tn),
        in_specs=[pl.BlockSpec((tm, k), lambda i, j: (i, 0)),
                  pl.BlockSpec((k, tn), lambda i, j: (0, j)),
                  pl.BlockSpec((tm, HEAD_DIM), lambda i, j: (i % t_per_seq, 0)),
                  pl.BlockSpec((tm, HEAD_DIM), lambda i, j: (i % t_per_seq, 0)),
                  pl.BlockSpec((1, tn), lambda i, j: (0, jnp.minimum(j, n_norm_tiles - 1)))],
        out_specs=pl.BlockSpec((tm, tn), lambda i, j: (i, j)),
        out_shape=jax.ShapeDtypeStruct((m, n), BF16),
        compiler_params=_cparams("arbitrary", "arbitrary"),
        name="matmul_qkvg",
    )(a, w, cos, sin, nw)


def _mm_ug_kernel(a_ref, wu_ref, wg_ref, *rest, tmm, n_side):
    side_in, o_ref, side_out = rest[:n_side], rest[n_side], rest[n_side + 1:]
    nbat = a_ref.shape[0]
    for src, dst in zip(side_in, side_out):
        dst[...] = src[...].astype(dst.dtype)
    for bi in range(nbat):
        a = a_ref[bi].reshape(SUBLANES * tmm, a_ref.shape[-1])
        u = jnp.dot(a, wu_ref[...], preferred_element_type=F32).astype(BF16).astype(F32)
        g = jnp.dot(a, wg_ref[...], preferred_element_type=F32).astype(BF16).astype(F32)
        word = lax.bitcast_convert_type(u, jnp.uint32) | (lax.bitcast_convert_type(g, jnp.uint32) >> 16)
        for seg in range(SUBLANES):
            for s in range(o_ref.shape[1]):
                o_ref[bi, s, pl.ds(seg, tmm, stride=SUBLANES), :] = (
                    word[seg * tmm:(seg + 1) * tmm, s * LANES:(s + 1) * LANES])


def _side_rows(rows, steps):
    cand = BF16_ROWS
    while rows % cand or rows // cand > steps:
        cand += BF16_ROWS
    return cand


def _mm_ug_call(h, w, r, side=()):
    b, l, k = h.shape
    lseg = l // SUBLANES
    tmm = _tile(lseg, 128, BF16_ROWS)
    nbat = _tile(b, max(1, 128 // tmm), 1)
    tn = _tile(r, 256, LANES)
    nslab = tn // LANES
    n_m = lseg // tmm
    grid = ((b // nbat) * n_m, r // tn)
    steps = grid[0] * grid[1]
    side_specs, side_shapes = [], []
    for t in side:
        rows = _side_rows(t.shape[0], steps)
        last = t.shape[0] // rows - 1
        spec = pl.BlockSpec((rows, t.shape[1]), lambda i, j, last=last: (jnp.minimum(i * grid[1] + j, last), 0))
        side_specs.append(spec)
        side_shapes.append(jax.ShapeDtypeStruct(t.shape, BF16))
    out = pl.pallas_call(
        functools.partial(_mm_ug_kernel, tmm=tmm, n_side=len(side)),
        grid=grid,
        in_specs=[pl.BlockSpec((nbat, SUBLANES, tmm, k), lambda i, j: (i // n_m, 0, i % n_m, 0)),
                  pl.BlockSpec((k, tn), lambda i, j: (0, j)),
                  pl.BlockSpec((k, tn), lambda i, j: (0, j + r // tn))] + side_specs,
        out_specs=[pl.BlockSpec((nbat, nslab, SUBLANES * tmm, LANES), lambda i, j: (i // n_m, j, i % n_m, 0))]
        + side_specs,
        out_shape=[jax.ShapeDtypeStruct((b, r // LANES, l, LANES), jnp.uint32)] + side_shapes,
        compiler_params=_cparams("arbitrary", "arbitrary"),
        name="matmul_ug",
    )(h.reshape(b, SUBLANES, lseg, k), w, w, *side)
    return out[0], out[1:]


def _mm_resid_kernel(a_ref, w_ref, x_ref, g_ref, o_ref, *, row, tiles_per_batch):
    r = (pl.program_id(0) // tiles_per_batch) if row is None else row
    y = jnp.dot(a_ref[...], w_ref[...], preferred_element_type=F32)
    o_ref[...] = x_ref[...] + g_ref[pl.ds(r, 1), :] * y


def _mm_resid_call(a, w, x, gate, rows_per_batch, row):
    m, k = a.shape
    n = w.shape[1]
    tm = _tile(rows_per_batch, 1024, BF16_ROWS)
    tn = _tile(n, 512, LANES)
    return pl.pallas_call(
        functools.partial(_mm_resid_kernel, row=row, tiles_per_batch=rows_per_batch // tm),
        grid=(m // tm, n // tn),
        in_specs=[pl.BlockSpec((tm, k), lambda i, j: (i, 0)),
                  pl.BlockSpec((k, tn), lambda i, j: (0, j)),
                  pl.BlockSpec((tm, tn), lambda i, j: (i, j)),
                  pl.BlockSpec((MOD_ROWS, tn), lambda i, j: (0, j))],
        out_specs=pl.BlockSpec((tm, tn), lambda i, j: (i, j)),
        out_shape=jax.ShapeDtypeStruct((m, n), F32),
        compiler_params=_cparams("arbitrary", "arbitrary"),
        name="matmul_resid",
    )(a, w, x, gate)


def _rglru_kernel(pl_ref, pc_ref, cw_ref, cb_ref, wg_ref, bg_ref, lam_ref, zl_ref, zc_ref,
                  cu_ref, hl_ref, pp_ref, zs_ref, *, s_len, c_len, bw):
    nslab = bw // LANES
    cw = cw_ref[...]
    cb = cb_ref[...]
    nlam = -lam_ref[...]
    softplus = jnp.maximum(nlam, 0.0) + jnp.log1p(jnp.exp(-jnp.abs(nlam)))
    k2 = (-0.5 * LRU_C * LOG2E) * softplus
    sub = lax.broadcasted_iota(jnp.int32, (SUBLANES, bw), 0)

    def unpack(p_ref, r0, n, low):
        parts = []
        for s in range(nslab):
            w = p_ref[s, pl.ds(r0, n), :]
            w = (w << 16) if low else (w & jnp.uint32(0xFFFF0000))
            parts.append(lax.bitcast_convert_type(w, F32))
        return jnp.concatenate(parts, axis=1)

    def seg_down(v):
        return jnp.where(sub == 0, 0.0, pltpu.roll(v, 1, 0))

    def seg_up(v):
        return jnp.where(sub == SUBLANES - 1, 0.0, pltpu.roll(v, SUBLANES - 1, 0))

    def conv_chunk(p_ref, k, n, n_chunks):
        length = n * n_chunks
        r0 = pl.multiple_of(k * n, n)
        lo = CONV_PAD_LO * SUBLANES
        hi = (CONV_W - 1 - CONV_PAD_LO) * SUBLANES
        cur = unpack(p_ref, r0, n, False)
        p0 = pl.multiple_of(jnp.where(k == 0, length - lo, r0 - lo), SUBLANES)
        n0 = pl.multiple_of(jnp.where(k == n_chunks - 1, 0, r0 + n), SUBLANES)
        prev = unpack(p_ref, p0, lo, False)
        nxt = unpack(p_ref, n0, hi, False)
        prev_edge = jnp.concatenate([seg_down(prev[i * SUBLANES:(i + 1) * SUBLANES])
                                     for i in range(CONV_PAD_LO)], axis=0)
        nxt_edge = jnp.concatenate([seg_up(nxt[i * SUBLANES:(i + 1) * SUBLANES])
                                    for i in range(CONV_W - 1 - CONV_PAD_LO)], axis=0)
        prev = jnp.where(k == 0, prev_edge, prev)
        nxt = jnp.where(k == n_chunks - 1, nxt_edge, nxt)
        ext = jnp.concatenate([prev, cur, nxt], axis=0)
        acc = cb + cw[0:1] * ext[0:n]
        for t in range(1, CONV_W):
            acc = acc + cw[t:t + 1] * ext[t * SUBLANES:t * SUBLANES + n]
        return acc

    def coeffs(uc, d):
        lo, hi = 2 * d * bw, 2 * (d + 1) * bw
        zg = jnp.dot(uc.astype(BF16), wg_ref[:, lo:hi], preferred_element_type=F32) + bg_ref[:, lo:hi]
        tr = jnp.tanh(zg[:, :bw])
        ti = jnp.tanh(zg[:, bw:])
        kd = k2[d:d + 1]
        la2 = kd * tr + kd
        a = jnp.exp2(la2)
        th = jnp.tanh(la2 * (-LN2))
        root = th * lax.rsqrt(jnp.maximum(th * th + th, F32_TINY))
        return a, root * ((ti + 1.0) * uc)

    def run_sequence(p_ref, z_ref, length, h0f, h0b):
        lseg = length // SUBLANES
        steps = _tile(lseg, SCAN_STEPS, BF16_ROWS)
        n = steps * SUBLANES
        n_chunks = length // n

        def conv_body(k, carry):
            cu_ref[pl.ds(pl.multiple_of(k * n, n), n), :] = conv_chunk(p_ref, k, n, n_chunks)
            return carry

        lax.fori_loop(0, n_chunks, conv_body, 0)

        def chunk_coeffs(k):
            return (coeffs(cu_ref[pl.ds(k * n, n), :], 0)
                    + coeffs(cu_ref[pl.ds((n_chunks - 1 - k) * n, n), :], 1))

        def chunk_scan(k, co, carry):
            af, bf, ab, bb = co
            hf, pf, hb, pb = carry
            rf = k * n
            rb = (n_chunks - 1 - k) * n
            hs_f, ps_f, hs_b, ps_b = [None] * steps, [None] * steps, [None] * steps, [None] * steps
            for m in range(steps):
                sf = slice(m * SUBLANES, (m + 1) * SUBLANES)
                hf = af[sf] * hf + bf[sf]
                pf = af[sf] * pf
                hs_f[m], ps_f[m] = hf, pf
                mb = steps - 1 - m
                sb = slice(mb * SUBLANES, (mb + 1) * SUBLANES)
                hb = ab[sb] * hb + bb[sb]
                pb = ab[sb] * pb
                hs_b[mb], ps_b[mb] = hb, pb
            hl_ref[0, pl.ds(rf, n), :] = jnp.concatenate(hs_f, axis=0)
            pp_ref[0, pl.ds(rf, n), :] = jnp.concatenate(ps_f, axis=0)
            hl_ref[1, pl.ds(rb, n), :] = jnp.concatenate(hs_b, axis=0)
            pp_ref[1, pl.ds(rb, n), :] = jnp.concatenate(ps_b, axis=0)
            return hf, pf, hb, pb

        carry = (jnp.zeros((SUBLANES, bw), F32), jnp.ones((SUBLANES, bw), F32),
                 jnp.zeros((SUBLANES, bw), F32), jnp.ones((SUBLANES, bw), F32))
        nxt = chunk_coeffs(0)
        for k in range(n_chunks):
            cur = nxt
            if k + 1 < n_chunks:
                nxt = chunk_coeffs(k + 1)
            carry = chunk_scan(k, cur, carry)
        ef, qf, eb, qb = carry

        rows_f, rows_b = [None] * SUBLANES, [None] * SUBLANES
        c = h0f
        for s in range(SUBLANES):
            rows_f[s] = c
            c = qf[s:s + 1] * c + ef[s:s + 1]
        hf_end = c
        c = h0b
        for s in range(SUBLANES - 1, -1, -1):
            rows_b[s] = c
            c = qb[s:s + 1] * c + eb[s:s + 1]
        hb_end = c
        cf = jnp.concatenate(rows_f, axis=0)[None]
        cbk = jnp.concatenate(rows_b, axis=0)[None]

        def out_body(k, carry):
            r0 = pl.multiple_of(k * n, n)
            sl = pl.ds(r0, n)
            hf = hl_ref[0, sl, :].reshape(steps, SUBLANES, bw) + pp_ref[0, sl, :].reshape(steps, SUBLANES, bw) * cf
            hb = hl_ref[1, sl, :].reshape(steps, SUBLANES, bw) + pp_ref[1, sl, :].reshape(steps, SUBLANES, bw) * cbk
            g = unpack(p_ref, r0, n, True)
            gate = (g * (0.25 * SQRT2)) * (jnp.tanh(g * 0.5) + 1.0)
            z = (hf + hb).reshape(n, bw) * gate
            for s in range(nslab):
                zs_ref[s, 0:n, :] = z[:, s * LANES:(s + 1) * LANES]
            for seg in range(SUBLANES):
                piece = jnp.concatenate([zs_ref[s, pl.ds(seg, steps, stride=SUBLANES), :]
                                         for s in range(nslab)], axis=1)
                t0 = pl.multiple_of(seg * lseg + k * steps, steps)
                z_ref[pl.ds(t0, steps), :] = piece.astype(z_ref.dtype)
            return carry

        lax.fori_loop(0, n_chunks, out_body, 0)
        return hf_end, hb_end

    zero = jnp.zeros((1, bw), F32)
    hf0, hb0 = run_sequence(pc_ref, zc_ref, c_len, zero, zero)
    run_sequence(pl_ref, zl_ref, s_len, hf0, hb0)


def _rglru_call(p_lat, p_ctx, conv_w, conv_b, w_gate, b_gate, lam):
    b, nsl, s_len, _ = p_lat.shape
    c_len = p_ctx.shape[2]
    r = nsl * LANES
    nblk, bw, _ = w_gate.shape
    nslab = bw // LANES
    lmax = max(s_len, c_len)
    chunk = SUBLANES * SCAN_STEPS
    kern = functools.partial(_rglru_kernel, s_len=s_len, c_len=c_len, bw=bw)
    return pl.pallas_call(
        kern,
        grid=(b, nblk),
        in_specs=[pl.BlockSpec((None, nslab, s_len, LANES), lambda i, j: (i, j, 0, 0)),
                  pl.BlockSpec((None, nslab, c_len, LANES), lambda i, j: (i, j, 0, 0)),
                  pl.BlockSpec((CONV_W, bw), lambda i, j: (0, j)),
                  pl.BlockSpec((1, bw), lambda i, j: (0, j)),
                  pl.BlockSpec((None, bw, 4 * bw), lambda i, j: (j, 0, 0)),
                  pl.BlockSpec((None, 1, 4 * bw), lambda i, j: (j, 0, 0)),
                  pl.BlockSpec((2, bw), lambda i, j: (0, j))],
        out_specs=[pl.BlockSpec((None, s_len, bw), lambda i, j: (i, 0, j)),
                   pl.BlockSpec((None, c_len, bw), lambda i, j: (i, 0, j))],
        out_shape=[jax.ShapeDtypeStruct((b, s_len, r), BF16),
                   jax.ShapeDtypeStruct((b, c_len, r), BF16)],
        scratch_shapes=[pltpu.VMEM((lmax, bw), F32),
                        pltpu.VMEM((2, lmax, bw), F32),
                        pltpu.VMEM((2, lmax, bw), F32),
                        pltpu.VMEM((nslab, chunk, LANES), F32)],
        compiler_params=_cparams("arbitrary", "arbitrary"),
        name="rglru",
    )(p_lat, p_ctx, conv_w, conv_b.reshape(1, r), w_gate, b_gate, lam)


def _headnorm_kernel(x_ref, w_ref, o_ref, *, n_heads):
    w = w_ref[...]
    for h in range(n_heads):
        sl = slice(h * HEAD_DIM, (h + 1) * HEAD_DIM)
        x = x_ref[:, sl].astype(F32)
        inv = lax.rsqrt(jnp.mean(x * x, axis=-1, keepdims=True) + NORM_EPS)
        o_ref[:, sl] = ((x * inv) * w).astype(o_ref.dtype)


def _qkprep_call(x, col0, width, w):
    b, l, _ = x.shape
    tr = _tile(l, 256, BF16_ROWS)
    assert col0 % width == 0
    off = col0 // width
    return pl.pallas_call(
        functools.partial(_headnorm_kernel, n_heads=width // HEAD_DIM),
        grid=(b, l // tr),
        in_specs=[pl.BlockSpec((None, tr, width), lambda i, j: (i, j, off)),
                  pl.BlockSpec((1, HEAD_DIM), lambda i, j: (0, 0))],
        out_specs=pl.BlockSpec((None, tr, width), lambda i, j: (i, j, 0)),
        out_shape=jax.ShapeDtypeStruct((b, l, width), BF16),
        compiler_params=_cparams("arbitrary", "arbitrary"),
        name="ctx_key_norm",
    )(x, w.reshape(1, HEAD_DIM))


def _dot_nt(a, b):
    return lax.dot_general(a, b, (((1,), (1,)), ((), ())), preferred_element_type=F32)


def _attn_kernel(sink_ref, bias_ref, q_ref, kp_ref, km_ref, kn_ref, vp_ref, vm_ref, vn_ref, kc_ref, vc_ref,
                 g_ref, o_ref, *, groups, qblocks, n_steps):
    kvh = pl.program_id(1)
    step = pl.program_id(2)
    blk = ATTN_BLOCK
    kw = jnp.concatenate([kp_ref[...], km_ref[...], kn_ref[...]], axis=0)
    vw = jnp.concatenate([vp_ref[...], vm_ref[...], vn_ref[...]], axis=0)
    kc = kc_ref[...]
    vt_c = jnp.concatenate([vc_ref[...].T, jnp.ones((BF16_ROWS, kc.shape[0]), BF16)], axis=0)
    vt_w = jnp.concatenate([vw.T, jnp.ones((BF16_ROWS, vw.shape[0]), BF16)], axis=0)
    sink2 =jnp.concatenate([jnp.full((1, blk), sink_ref[kvh * groups + g] * LOG2E, F32)
                             for g in range(groups)], axis=1)
    def scores(i):
        q = q_ref[i * blk:(i + 1) * blk, :]
        qs = jnp.concatenate([q[:, g * HEAD_DIM:(g + 1) * HEAD_DIM] for g in range(groups)], axis=0)
        first = (step == 0).astype(jnp.int32) if i == 0 else 0
        last = (step == n_steps - 1).astype(jnp.int32) if i == qblocks - 1 else 0
        bias = bias_ref[first + 2 * last]
        return _dot_nt(kc, qs), _dot_nt(kw[i * blk:(i + 3) * blk], qs) + bias

    ahead = [scores(i) for i in range(min(SCORE_LOOKAHEAD, qblocks))]
    for i in range(qblocks):
        s_c, s_w = ahead.pop(0)
        if i + SCORE_LOOKAHEAD < qblocks:
            ahead.append(scores(i + SCORE_LOOKAHEAD))
        m = jnp.maximum(jnp.maximum(jnp.max(s_c, axis=0, keepdims=True),
                                    jnp.max(s_w, axis=0, keepdims=True)), sink2)
        p_c = jnp.exp2(s_c - m).astype(BF16)
        p_w = jnp.exp2(s_w - m).astype(BF16)
        ot = (jnp.dot(vt_c, p_c, preferred_element_type=F32)
              + jnp.dot(vt_w[:, i * blk:(i + 3) * blk], p_w, preferred_element_type=F32))
        den = ot[HEAD_DIM:HEAD_DIM + 1, :] + jnp.exp2(sink2 - m)
        ot = ot[:HEAD_DIM, :] * (1.0 / den)
        o = jnp.concatenate([ot[:, g * blk:(g + 1) * blk].T for g in range(groups)], axis=1)
        gate = _silu(g_ref[i * blk:(i + 1) * blk, :].astype(F32))
        o_ref[i * blk:(i + 1) * blk, :] = (o * gate).astype(o_ref.dtype)


def _band_bias(groups):
    blk = ATTN_BLOCK
    ki = jnp.arange(3 * blk)[:, None]
    qi = jnp.arange(blk)[None, :]
    band = jnp.abs(ki - blk - qi) <= WINDOW
    out = []
    for var in range(4):
        ok = band & ((ki >= blk) | (var % 2 == 0)) & ((ki < 2 * blk) | (var // 2 == 0))
        out.append(jnp.tile(jnp.where(ok, 0.0, NEG_INF).astype(F32), (1, groups)))
    return jnp.stack(out)


def _attn_call(sink, kcn, qkvg, kvc, attn_w, kv_w):
    b, s_len, _ = qkvg.shape
    c_len = kcn.shape[1]
    n_kv = kv_w // HEAD_DIM
    groups = attn_w // kv_w
    gw = groups * HEAD_DIM
    blk = ATTN_BLOCK
    nb = s_len // blk
    qblocks = next(q for q in (8, 4, 2, 1) if nb % q == 0)
    n_steps = nb // qblocks
    k_off = attn_w // HEAD_DIM
    v_off = (attn_w + kv_w) // HEAD_DIM
    g_off = (attn_w + 2 * kv_w) // gw
    assert (attn_w + 2 * kv_w) % gw == 0

    def edge_spec(prev, off):
        if prev:
            return pl.BlockSpec((None, blk, HEAD_DIM),
                                lambda i, h, j: (i, jnp.maximum(j * qblocks - 1, 0), h + off))
        return pl.BlockSpec((None, blk, HEAD_DIM),
                            lambda i, h, j: (i, jnp.minimum((j + 1) * qblocks, nb - 1), h + off))

    def main_spec(off):
        return pl.BlockSpec((None, qblocks * blk, HEAD_DIM), lambda i, h, j: (i, j, h + off))

    def tile_spec(off):
        return pl.BlockSpec((None, qblocks * blk, gw), lambda i, h, j: (i, j, h + off))

    return pl.pallas_call(
        functools.partial(_attn_kernel, groups=groups, qblocks=qblocks, n_steps=n_steps),
        grid=(b, n_kv, n_steps),
        in_specs=[pl.BlockSpec(memory_space=pltpu.SMEM),
                  pl.BlockSpec((4, 3 * blk, groups * blk), lambda i, h, j: (0, 0, 0)),
                  tile_spec(0),
                  edge_spec(True, k_off), main_spec(k_off), edge_spec(False, k_off),
                  edge_spec(True, v_off), main_spec(v_off), edge_spec(False, v_off),
                  pl.BlockSpec((None, c_len, HEAD_DIM), lambda i, h, j: (i, 0, h)),
                  pl.BlockSpec((None, c_len, HEAD_DIM), lambda i, h, j: (i, 0, h + n_kv)),
                  tile_spec(g_off)],
        out_specs=tile_spec(0),
        out_shape=jax.ShapeDtypeStruct((b, s_len, attn_w), BF16),
        compiler_params=_cparams("arbitrary", "arbitrary", "arbitrary"),
        name="attention",
    )(sink, _band_bias(groups), qkvg, qkvg, qkvg, qkvg, qkvg, qkvg, qkvg, kcn, kvc, qkvg)


def _rope_tables(s_len):
    t = jnp.arange(s_len, dtype=jnp.int32)
    pos = jnp.stack([t // GRID_W, t % GRID_W], axis=1).astype(F32)
    half = HEAD_DIM // 2
    quarter = half // 2
    inv_freq = ROPE_BASE ** (-jnp.arange(quarter, dtype=F32) * (2.0 / half))
    ang = pos[:, :, None] * inv_freq[None, None, :]
    cos = jnp.concatenate([jnp.cos(ang), jnp.cos(ang)], axis=-1).reshape(s_len, HEAD_DIM)
    sin = jnp.concatenate([-jnp.sin(ang), jnp.sin(ang)], axis=-1).reshape(s_len, HEAD_DIM)
    return cos, sin


def kernel(x, c, ctx, c_ctx, w_mod, b_mod, norm_w, rg_w_in, rg_conv_w, rg_conv_b, rg_w_r, rg_b_r, rg_w_i,
           rg_b_i, rg_lam, rg_w_out, at_w_in, at_q_norm, at_k_norm, at_sink, at_w_out):
    b, s_len, d = x.shape
    c_len = ctx.shape[1]
    depth = w_mod.shape[0]
    assert depth == 2 and b < MOD_ROWS
    ctx_row = b

    c8 = jnp.concatenate([c, c_ctx[None, :], jnp.zeros((MOD_ROWS - b - 1, d), F32)], axis=0)
    mod = _mod_call(c8, w_mod, b_mod)
    shift, scale, gate = mod[:, :, :d], mod[:, :, d:2 * d], mod[:, :, 2 * d:]

    r = rg_w_out.shape[1]
    h = _adaln_call(x, shift[0], scale[0], norm_w[0], None)
    hc = _adaln_call(ctx, shift[0], scale[0], norm_w[0], ctx_row)
    w_in = rg_w_in[0].astype(BF16)
    p_lat, (w_out, at_w_in_bf, at_w_out_bf) = _mm_ug_call(h, w_in, r, (rg_w_out[0], at_w_in[0], at_w_out[0]))
    p_ctx, _ = _mm_ug_call(hc, w_in, r)
    w_gate = (0.5 * jnp.concatenate([rg_w_r[0, 0], rg_w_i[0, 0], rg_w_r[0, 1], rg_w_i[0, 1]], axis=-1)).astype(BF16)
    nblk, bw = rg_w_r.shape[2], rg_w_r.shape[3]
    b_gate = 0.5 * jnp.concatenate([rg_b_r[0, 0].reshape(nblk, 1, bw), rg_b_i[0, 0].reshape(nblk, 1, bw),
                                    rg_b_r[0, 1].reshape(nblk, 1, bw), rg_b_i[0, 1].reshape(nblk, 1, bw)], axis=-1)
    z, zc = _rglru_call(p_lat, p_ctx, rg_conv_w[0], rg_conv_b[0], w_gate, b_gate, rg_lam[0])
    x = _mm_resid_call(z.reshape(b * s_len, r), w_out, x.reshape(b * s_len, d), gate[0], s_len, None)
    x = x.reshape(b, s_len, d)
    ctx = _mm_resid_call(zc.reshape(b * c_len, r), w_out, ctx.reshape(b * c_len, d), gate[0], c_len, ctx_row)
    ctx = ctx.reshape(b, c_len, d)

    attn_w = at_w_out.shape[1]
    kv_w = (at_w_in.shape[2] - 2 * attn_w) // 2
    h = _adaln_call(x, shift[1], scale[1], norm_w[1], None)
    hc = _adaln_call(ctx, shift[1], scale[1], norm_w[1], ctx_row)
    n_heads, n_kv = attn_w // HEAD_DIM, kv_w // HEAD_DIM
    w_in = at_w_in_bf
    k_norm = at_k_norm[0]
    nw = jnp.concatenate([jnp.tile(at_q_norm[0] * SCORE_SCALE, n_heads), jnp.tile(k_norm, n_kv)])
    nw = nw.reshape(1, attn_w + kv_w)
    cos, sin = _rope_tables(s_len)
    qkvg = _mm_qkvg_call(h.reshape(b * s_len, d), w_in, cos, sin, nw, s_len).reshape(b, s_len, -1)
    kvc = _mm_call(hc.reshape(b * c_len, d), w_in, attn_w, 2 * kv_w, BF16).reshape(b, c_len, 2 * kv_w)
    kcn = _qkprep_call(kvc, 0, kv_w, k_norm)
    z = _attn_call(at_sink[0], kcn, qkvg, kvc, attn_w, kv_w)
    x = _mm_resid_call(z.reshape(b * s_len, attn_w), at_w_out_bf, x.reshape(b * s_len, d),
                       gate[1], s_len, None)
    return x.reshape(b, s_len, d)
```

```python
import functools

import jax
import jax.numpy as jnp
from jax import lax
from jax.experimental import pallas as pl
from jax.experimental.pallas import tpu as pltpu

F32 = jnp.float32
BF16 = jnp.bfloat16

HEAD_DIM = 128
WINDOW = 128
ATTN_BLOCK = 128
GRID_W = 64
ROPE_BASE = 10000.0
NORM_EPS = 1e-6
NEG_INF = -1e30
LRU_C = 8.0
CONV_W = 4
CONV_PAD_LO = 2
LOG2E = 1.4426950408889634
LN2 = 0.6931471805599453
SCORE_SCALE = HEAD_DIM ** -0.5 * LOG2E
SQRT2 = 1.4142135623730951
F32_TINY = 1.1754943508222875e-38

SUBLANES = 8
LANES = 128
BF16_ROWS = 16
MOD_ROWS = 8
VMEM_LIMIT = 56 * 1024 * 1024
SCAN_STEPS = 64
NORM_ROW_BLOCKS = 4
ADALN_SEGS = 2
SCORE_LOOKAHEAD = 2


def _cparams(*sem):
    return pltpu.CompilerParams(dimension_semantics=sem, vmem_limit_bytes=VMEM_LIMIT)


def _tile(n, pref, unit):
    if n <= pref:
        return n
    t = (pref // unit) * unit
    while n % t:
        t -= unit
    return t


def _sigmoid(x):
    return 1.0 / (1.0 + jnp.exp(-x))


def _silu(x):
    return x * _sigmoid(x)


def _mod_kernel(c_ref, w_ref, b_ref, o_ref):
    s = _silu(c_ref[...]).astype(BF16)
    o_ref[...] = jnp.dot(s, w_ref[...].astype(BF16), preferred_element_type=F32) + b_ref[...]


def _mod_call(c8, w_mod, b_mod):
    depth, d, n3 = w_mod.shape
    tn = _tile(n3, 768, LANES)
    return pl.pallas_call(
        _mod_kernel,
        grid=(depth, n3 // tn),
        in_specs=[pl.BlockSpec((MOD_ROWS, d), lambda l, j: (0, 0)),
                  pl.BlockSpec((None, d, tn), lambda l, j: (l, 0, j)),
                  pl.BlockSpec((None, 1, tn), lambda l, j: (l, 0, j))],
        out_specs=pl.BlockSpec((None, MOD_ROWS, tn), lambda l, j: (l, 0, j)),
        out_shape=jax.ShapeDtypeStruct((depth, MOD_ROWS, n3), F32),
        compiler_params=_cparams("arbitrary", "arbitrary"),
        name="mod",
    )(c8, w_mod, b_mod.reshape(depth, 1, n3))


def _adaln_kernel(x_ref, sh_ref, sc_ref, nw_ref, o_ref, *, row):
    r = pl.program_id(0) if row is None else row
    x = x_ref[...]
    inv = lax.rsqrt(jnp.mean(x * x, axis=-1, keepdims=True) + NORM_EPS)
    xn = (x * inv) * nw_ref[...]
    o_ref[...] = (xn * (1.0 + sc_ref[pl.ds(r, 1), :]) + sh_ref[pl.ds(r, 1), :]).astype(o_ref.dtype)


def _adaln_call(x, shift, scale, nw, row):
    b, l, d = x.shape
    tr = _tile(l, 512, BF16_ROWS)
    return pl.pallas_call(
        functools.partial(_adaln_kernel, row=row),
        grid=(b, l // tr),
        in_specs=[pl.BlockSpec((None, tr, d), lambda i, j: (i, j, 0)),
                  pl.BlockSpec((MOD_ROWS, d), lambda i, j: (0, 0)),
                  pl.BlockSpec((MOD_ROWS, d), lambda i, j: (0, 0)),
                  pl.BlockSpec((1, d), lambda i, j: (0, 0))],
        out_specs=pl.BlockSpec((None, tr, d), lambda i, j: (i, j, 0)),
        out_shape=jax.ShapeDtypeStruct((b, l, d), BF16),
        compiler_params=_cparams("arbitrary", "arbitrary"),
        name="adaln",
    )(x, shift, scale, nw.reshape(1, d))


def _mm_kernel(a_ref, w_ref, o_ref):
    o_ref[...] = jnp.dot(a_ref[...], w_ref[...], preferred_element_type=F32).astype(o_ref.dtype)


def _mm_call(a, w, col0, ncols, out_dtype):
    m, k = a.shape
    tm = _tile(m, 1024, BF16_ROWS)
    tn = _tile(ncols, 512, LANES)
    assert col0 % tn == 0
    off = col0 // tn
    return pl.pallas_call(
        _mm_kernel,
        grid=(m // tm, ncols // tn),
        in_specs=[pl.BlockSpec((tm, k), lambda i, j: (i, 0)),
                  pl.BlockSpec((k, tn), lambda i, j: (0, j + off))],
        out_specs=pl.BlockSpec((tm, tn), lambda i, j: (i, j)),
        out_shape=jax.ShapeDtypeStruct((m, ncols), out_dtype),
        compiler_params=_cparams("arbitrary", "arbitrary"),
        name="matmul",
    )(a, w)


def _adaln_rows(x, sh_ref, sc_ref, nwk_ref, r):
    inv = lax.rsqrt(jnp.mean(x * x, axis=-1, keepdims=True) + NORM_EPS)
    xn = (x * inv) * nwk_ref[...]
    return (xn * (1.0 + sc_ref[pl.ds(r, 1), :]) + sh_ref[pl.ds(r, 1), :]).astype(BF16)


def _mm_qkvg_kernel(x_ref, sh_ref, sc_ref, nwk_ref, w_ref, cos_ref, sin_ref, nw_ref, o_ref, h_ref, *,
                    n_norm_tiles, tiles_per_seq):
    j = pl.program_id(1)
    tm = x_ref.shape[0]
    rb = tm // NORM_ROW_BLOCKS if tm % (NORM_ROW_BLOCKS * BF16_ROWS) == 0 else tm
    quarter = HEAD_DIM // 4

    def qk_rows(r0, a):
        y = jnp.dot(a, w_ref[...], preferred_element_type=F32)
        cos = cos_ref[r0:r0 + rb, :]
        sin = sin_ref[r0:r0 + rb, :]
        lane = lax.broadcasted_iota(jnp.int32, (rb, HEAD_DIM), 1)
        x1_lanes = (lane % (2 * quarter)) < quarter
        for h in range(y.shape[1] // HEAD_DIM):
            sl = slice(h * HEAD_DIM, (h + 1) * HEAD_DIM)
            x = y[:, sl]
            inv = lax.rsqrt(jnp.mean(x * x, axis=-1, keepdims=True) + NORM_EPS)
            xn = (x * inv) * nw_ref[:, sl]
            partner = jnp.where(x1_lanes, pltpu.roll(xn, HEAD_DIM - quarter, 1), pltpu.roll(xn, quarter, 1))
            o_ref[r0:r0 + rb, sl] = (xn * cos + partner * sin).astype(o_ref.dtype)

    @pl.when(j == 0)
    def _():
        r = pl.program_id(0) // tiles_per_seq
        for r0 in range(0, tm, rb):
            a = _adaln_rows(x_ref[r0:r0 + rb, :], sh_ref, sc_ref, nwk_ref, r)
            h_ref[r0:r0 + rb, :] = a
            qk_rows(r0, a)

    @pl.when((j > 0) & (j < n_norm_tiles))
    def _():
        for r0 in range(0, tm, rb):
            qk_rows(r0, h_ref[r0:r0 + rb, :])

    @pl.when(j >= n_norm_tiles)
    def _():
        o_ref[...] = jnp.dot(h_ref[...], w_ref[...], preferred_element_type=F32).astype(o_ref.dtype)


def _mm_qkvg_call(x, shift, scale, nwk, w, cos, sin, nw):
    b, s_len, k = x.shape
    m = b * s_len
    n = w.shape[1]
    tm = _tile(s_len, 512, BF16_ROWS)
    tn = _tile(nw.shape[1], 1024, HEAD_DIM)
    assert n % tn == 0 and nw.shape[1] >= tn
    n_norm_tiles = nw.shape[1] // tn
    t_per_seq = s_len // tm
    return pl.pallas_call(
        functools.partial(_mm_qkvg_kernel, n_norm_tiles=n_norm_tiles, tiles_per_seq=t_per_seq),
        grid=(m // tm, n // tn),
        in_specs=[pl.BlockSpec((tm, k), lambda i, j: (i, 0)),
                  pl.BlockSpec((MOD_ROWS, k), lambda i, j: (0, 0)),
                  pl.BlockSpec((MOD_ROWS, k), lambda i, j: (0, 0)),
                  pl.BlockSpec((1, k), lambda i, j: (0, 0)),
                  pl.BlockSpec((k, tn), lambda i, j: (0, j)),
                  pl.BlockSpec((tm, HEAD_DIM), lambda i, j: (i % t_per_seq, 0)),
                  pl.BlockSpec((tm, HEAD_DIM), lambda i, j: (i % t_per_seq, 0)),
                  pl.BlockSpec((1, tn), lambda i, j: (0, jnp.minimum(j, n_norm_tiles - 1)))],
        out_specs=pl.BlockSpec((tm, tn), lambda i, j: (i, j)),
        out_shape=jax.ShapeDtypeStruct((m, n), BF16),
        scratch_shapes=[pltpu.VMEM((tm, k), BF16)],
        compiler_params=_cparams("arbitrary", "arbitrary"),
        name="matmul_qkvg",
    )(x.reshape(m, k), shift, scale, nwk.reshape(1, k), w, cos, sin, nw)


def _mm_ug_kernel(x_ref, sh_ref, sc_ref, nwk_ref, wu_ref, wg_ref, *rest, tmm, n_side, row, n_m):
    side_in, o_ref, side_out, h_ref = rest[:n_side], rest[n_side], rest[n_side + 1:-1], rest[-1]
    nbat, _, _, k = x_ref.shape
    j = pl.program_id(1)
    for src, dst in zip(side_in, side_out):
        dst[...] = src[...].astype(dst.dtype)

    def project(bi, seg0, nseg, a):
        u = jnp.dot(a, wu_ref[...], preferred_element_type=F32).astype(BF16).astype(F32)
        g = jnp.dot(a, wg_ref[...], preferred_element_type=F32).astype(BF16).astype(F32)
        word = lax.bitcast_convert_type(u, jnp.uint32) | (lax.bitcast_convert_type(g, jnp.uint32) >> 16)
        for ds in range(nseg):
            for s in range(o_ref.shape[1]):
                o_ref[bi, s, pl.ds(seg0 + ds, tmm, stride=SUBLANES), :] = (
                    word[ds * tmm:(ds + 1) * tmm, s * LANES:(s + 1) * LANES])

    @pl.when(j == 0)
    def _():
        for bi in range(nbat):
            r = ((pl.program_id(0) // n_m) * nbat + bi) if row is None else row
            for seg0 in range(0, SUBLANES, ADALN_SEGS):
                x = x_ref[bi, seg0:seg0 + ADALN_SEGS].reshape(ADALN_SEGS * tmm, k)
                a = _adaln_rows(x, sh_ref, sc_ref, nwk_ref, r)
                h_ref[bi, seg0:seg0 + ADALN_SEGS] = a.reshape(ADALN_SEGS, tmm, k)
                project(bi, seg0, ADALN_SEGS, a)

    @pl.when(j > 0)
    def _():
        for bi in range(nbat):
            project(bi, 0, SUBLANES, h_ref[bi].reshape(SUBLANES * tmm, k))


def _side_rows(rows, steps):
    cand = BF16_ROWS
    while rows % cand or rows // cand > steps:
        cand += BF16_ROWS
    return cand


def _mm_ug_call(x, shift, scale, nwk, row, w, r, side=()):
    b, l, k = x.shape
    lseg = l // SUBLANES
    tmm = _tile(lseg, 64, BF16_ROWS)
    nbat = _tile(b, max(1, 64 // tmm), 1)
    tn = _tile(r, 512, LANES)
    nslab = tn // LANES
    n_m = lseg // tmm
    grid = ((b // nbat) * n_m, r // tn)
    steps = grid[0] * grid[1]
    side_specs, side_shapes = [], []
    for t in side:
        rows = _side_rows(t.shape[0], steps)
        last = t.shape[0] // rows - 1
        spec = pl.BlockSpec((rows, t.shape[1]), lambda i, j, last=last: (jnp.minimum(i * grid[1] + j, last), 0))
        side_specs.append(spec)
        side_shapes.append(jax.ShapeDtypeStruct(t.shape, BF16))
    out = pl.pallas_call(
        functools.partial(_mm_ug_kernel, tmm=tmm, n_side=len(side), row=row, n_m=n_m),
        grid=grid,
        in_specs=[pl.BlockSpec((nbat, SUBLANES, tmm, k), lambda i, j: (i // n_m, 0, i % n_m, 0)),
                  pl.BlockSpec((MOD_ROWS, k), lambda i, j: (0, 0)),
                  pl.BlockSpec((MOD_ROWS, k), lambda i, j: (0, 0)),
                  pl.BlockSpec((1, k), lambda i, j: (0, 0)),
                  pl.BlockSpec((k, tn), lambda i, j: (0, j)),
                  pl.BlockSpec((k, tn), lambda i, j: (0, j + r // tn))] + side_specs,
        out_specs=[pl.BlockSpec((nbat, nslab, SUBLANES * tmm, LANES), lambda i, j: (i // n_m, j, i % n_m, 0))]
        + side_specs,
        out_shape=[jax.ShapeDtypeStruct((b, r // LANES, l, LANES), jnp.uint32)] + side_shapes,
        scratch_shapes=[pltpu.VMEM((nbat, SUBLANES, tmm, k), BF16)],
        compiler_params=_cparams("arbitrary", "arbitrary"),
        name="matmul_ug",
    )(x.reshape(b, SUBLANES, lseg, k), shift, scale, nwk.reshape(1, k), w, w, *side)
    return out[0], out[1:]


def _mm_resid_kernel(a_ref, w_ref, x_ref, g_ref, o_ref, *, row, tiles_per_batch):
    r = (pl.program_id(0) // tiles_per_batch) if row is None else row
    y = jnp.dot(a_ref[...], w_ref[...], preferred_element_type=F32)
    o_ref[...] = x_ref[...] + g_ref[pl.ds(r, 1), :] * y


def _mm_resid_call(a, w, x, gate, rows_per_batch, row):
    m, k = a.shape
    n = w.shape[1]
    tm = _tile(rows_per_batch, 1024, BF16_ROWS)
    tn = _tile(n, 512, LANES)
    return pl.pallas_call(
        functools.partial(_mm_resid_kernel, row=row, tiles_per_batch=rows_per_batch // tm),
        grid=(m // tm, n // tn),
        in_specs=[pl.BlockSpec((tm, k), lambda i, j: (i, 0)),
                  pl.BlockSpec((k, tn), lambda i, j: (0, j)),
                  pl.BlockSpec((tm, tn), lambda i, j: (i, j)),
                  pl.BlockSpec((MOD_ROWS, tn), lambda i, j: (0, j))],
        out_specs=pl.BlockSpec((tm, tn), lambda i, j: (i, j)),
        out_shape=jax.ShapeDtypeStruct((m, n), F32),
        compiler_params=_cparams("arbitrary", "arbitrary"),
        name="matmul_resid",
    )(a, w, x, gate)


def _rglru_kernel(pl_ref, pc_ref, cw_ref, cb_ref, wg_ref, bg_ref, lam_ref, zl_ref, zc_ref,
                  cu_ref, hl_ref, pp_ref, zs_ref, *, s_len, c_len, bw):
    nslab = bw // LANES
    cw = cw_ref[...]
    cb = cb_ref[...]
    nlam = -lam_ref[...]
    softplus = jnp.maximum(nlam, 0.0) + jnp.log1p(jnp.exp(-jnp.abs(nlam)))
    k2 = (-0.5 * LRU_C * LOG2E) * softplus
    sub = lax.broadcasted_iota(jnp.int32, (SUBLANES, bw), 0)

    def unpack(p_ref, r0, n, low):
        parts = []
        for s in range(nslab):
            w = p_ref[s, pl.ds(r0, n), :]
            w = (w << 16) if low else (w & jnp.uint32(0xFFFF0000))
            parts.append(lax.bitcast_convert_type(w, F32))
        return jnp.concatenate(parts, axis=1)

    def seg_down(v):
        return jnp.where(sub == 0, 0.0, pltpu.roll(v, 1, 0))

    def seg_up(v):
        return jnp.where(sub == SUBLANES - 1, 0.0, pltpu.roll(v, SUBLANES - 1, 0))

    def conv_chunk(p_ref, k, n, n_chunks):
        length = n * n_chunks
        r0 = pl.multiple_of(k * n, n)
        lo = CONV_PAD_LO * SUBLANES
        hi = (CONV_W - 1 - CONV_PAD_LO) * SUBLANES
        cur = unpack(p_ref, r0, n, False)
        p0 = pl.multiple_of(jnp.where(k == 0, length - lo, r0 - lo), SUBLANES)
        n0 = pl.multiple_of(jnp.where(k == n_chunks - 1, 0, r0 + n), SUBLANES)
        prev = unpack(p_ref, p0, lo, False)
        nxt = unpack(p_ref, n0, hi, False)
        prev_edge = jnp.concatenate([seg_down(prev[i * SUBLANES:(i + 1) * SUBLANES])
                                     for i in range(CONV_PAD_LO)], axis=0)
        nxt_edge = jnp.concatenate([seg_up(nxt[i * SUBLANES:(i + 1) * SUBLANES])
                                    for i in range(CONV_W - 1 - CONV_PAD_LO)], axis=0)
        prev = jnp.where(k == 0, prev_edge, prev)
        nxt = jnp.where(k == n_chunks - 1, nxt_edge, nxt)
        ext = jnp.concatenate([prev, cur, nxt], axis=0)
        acc = cb + cw[0:1] * ext[0:n]
        for t in range(1, CONV_W):
            acc = acc + cw[t:t + 1] * ext[t * SUBLANES:t * SUBLANES + n]
        return acc

    def coeffs(uc, d):
        lo, hi = 2 * d * bw, 2 * (d + 1) * bw
        zg = jnp.dot(uc.astype(BF16), wg_ref[:, lo:hi], preferred_element_type=F32) + bg_ref[:, lo:hi]
        tr = jnp.tanh(zg[:, :bw])
        ti = jnp.tanh(zg[:, bw:])
        kd = k2[d:d + 1]
        la2 = kd * tr + kd
        a = jnp.exp2(la2)
        th = jnp.tanh(la2 * (-LN2))
        root = th * lax.rsqrt(jnp.maximum(th * th + th, F32_TINY))
        return a, root * ((ti + 1.0) * uc)

    def run_sequence(p_ref, z_ref, length, h0f, h0b):
        lseg = length // SUBLANES
        steps = _tile(lseg, SCAN_STEPS, BF16_ROWS)
        n = steps * SUBLANES
        n_chunks = length // n

        def conv_body(k, carry):
            cu_ref[pl.ds(pl.multiple_of(k * n, n), n), :] = conv_chunk(p_ref, k, n, n_chunks)
            return carry

        lax.fori_loop(0, n_chunks, conv_body, 0)

        def chunk_coeffs(k):
            return (coeffs(cu_ref[pl.ds(k * n, n), :], 0)
                    + coeffs(cu_ref[pl.ds((n_chunks - 1 - k) * n, n), :], 1))

        def chunk_scan(k, co, carry):
            af, bf, ab, bb = co
            hf, pf, hb, pb = carry
            rf = k * n
            rb = (n_chunks - 1 - k) * n
            hs_f, ps_f, hs_b, ps_b = [None] * steps, [None] * steps, [None] * steps, [None] * steps
            for m in range(steps):
                sf = slice(m * SUBLANES, (m + 1) * SUBLANES)
                hf = af[sf] * hf + bf[sf]
                pf = af[sf] * pf
                hs_f[m], ps_f[m] = hf, pf
                mb = steps - 1 - m
                sb = slice(mb * SUBLANES, (mb + 1) * SUBLANES)
                hb = ab[sb] * hb + bb[sb]
                pb = ab[sb] * pb
                hs_b[mb], ps_b[mb] = hb, pb
            hl_ref[0, pl.ds(rf, n), :] = jnp.concatenate(hs_f, axis=0)
            pp_ref[0, pl.ds(rf, n), :] = jnp.concatenate(ps_f, axis=0)
            hl_ref[1, pl.ds(rb, n), :] = jnp.concatenate(hs_b, axis=0)
            pp_ref[1, pl.ds(rb, n), :] = jnp.concatenate(ps_b, axis=0)
            return hf, pf, hb, pb

        carry = (jnp.zeros((SUBLANES, bw), F32), jnp.ones((SUBLANES, bw), F32),
                 jnp.zeros((SUBLANES, bw), F32), jnp.ones((SUBLANES, bw), F32))
        nxt = chunk_coeffs(0)
        for k in range(n_chunks):
            cur = nxt
            if k + 1 < n_chunks:
                nxt = chunk_coeffs(k + 1)
            carry = chunk_scan(k, cur, carry)
        ef, qf, eb, qb = carry

        rows_f, rows_b = [None] * SUBLANES, [None] * SUBLANES
        c = h0f
        for s in range(SUBLANES):
            rows_f[s] = c
            c = qf[s:s + 1] * c + ef[s:s + 1]
        hf_end = c
        c = h0b
        for s in range(SUBLANES - 1, -1, -1):
            rows_b[s] = c
            c = qb[s:s + 1] * c + eb[s:s + 1]
        hb_end = c
        cf = jnp.concatenate(rows_f, axis=0)[None]
        cbk = jnp.concatenate(rows_b, axis=0)[None]

        def out_body(k, carry):
            r0 = pl.multiple_of(k * n, n)
            sl = pl.ds(r0, n)
            hf = hl_ref[0, sl, :].reshape(steps, SUBLANES, bw) + pp_ref[0, sl, :].reshape(steps, SUBLANES, bw) * cf
            hb = hl_ref[1, sl, :].reshape(steps, SUBLANES, bw) + pp_ref[1, sl, :].reshape(steps, SUBLANES, bw) * cbk
            g = unpack(p_ref, r0, n, True)
            gate = (g * (0.25 * SQRT2)) * (jnp.tanh(g * 0.5) + 1.0)
            z = (hf + hb).reshape(n, bw) * gate
            for s in range(nslab):
                zs_ref[s, 0:n, :] = z[:, s * LANES:(s + 1) * LANES]
            for seg in range(SUBLANES):
                piece = jnp.concatenate([zs_ref[s, pl.ds(seg, steps, stride=SUBLANES), :]
                                         for s in range(nslab)], axis=1)
                t0 = pl.multiple_of(seg * lseg + k * steps, steps)
                z_ref[pl.ds(t0, steps), :] = piece.astype(z_ref.dtype)
            return carry

        lax.fori_loop(0, n_chunks, out_body, 0)
        return hf_end, hb_end

    zero = jnp.zeros((1, bw), F32)
    hf0, hb0 = run_sequence(pc_ref, zc_ref, c_len, zero, zero)
    run_sequence(pl_ref, zl_ref, s_len, hf0, hb0)


def _rglru_call(p_lat, p_ctx, conv_w, conv_b, w_gate, b_gate, lam):
    b, nsl, s_len, _ = p_lat.shape
    c_len = p_ctx.shape[2]
    r = nsl * LANES
    nblk, bw, _ = w_gate.shape
    nslab = bw // LANES
    lmax = max(s_len, c_len)
    chunk = SUBLANES * SCAN_STEPS
    kern = functools.partial(_rglru_kernel, s_len=s_len, c_len=c_len, bw=bw)
    return pl.pallas_call(
        kern,
        grid=(b, nblk),
        in_specs=[pl.BlockSpec((None, nslab, s_len, LANES), lambda i, j: (i, j, 0, 0)),
                  pl.BlockSpec((None, nslab, c_len, LANES), lambda i, j: (i, j, 0, 0)),
                  pl.BlockSpec((CONV_W, bw), lambda i, j: (0, j)),
                  pl.BlockSpec((1, bw), lambda i, j: (0, j)),
                  pl.BlockSpec((None, bw, 4 * bw), lambda i, j: (j, 0, 0)),
                  pl.BlockSpec((None, 1, 4 * bw), lambda i, j: (j, 0, 0)),
                  pl.BlockSpec((2, bw), lambda i, j: (0, j))],
        out_specs=[pl.BlockSpec((None, s_len, bw), lambda i, j: (i, 0, j)),
                   pl.BlockSpec((None, c_len, bw), lambda i, j: (i, 0, j))],
        out_shape=[jax.ShapeDtypeStruct((b, s_len, r), BF16),
                   jax.ShapeDtypeStruct((b, c_len, r), BF16)],
        scratch_shapes=[pltpu.VMEM((lmax, bw), F32),
                        pltpu.VMEM((2, lmax, bw), F32),
                        pltpu.VMEM((2, lmax, bw), F32),
                        pltpu.VMEM((nslab, chunk, LANES), F32)],
        compiler_params=_cparams("arbitrary", "arbitrary"),
        name="rglru",
    )(p_lat, p_ctx, conv_w, conv_b.reshape(1, r), w_gate, b_gate, lam)


def _headnorm_kernel(x_ref, w_ref, o_ref, *, n_heads):
    w = w_ref[...]
    for h in range(n_heads):
        sl = slice(h * HEAD_DIM, (h + 1) * HEAD_DIM)
        x = x_ref[:, sl].astype(F32)
        inv = lax.rsqrt(jnp.mean(x * x, axis=-1, keepdims=True) + NORM_EPS)
        o_ref[:, sl] = ((x * inv) * w).astype(o_ref.dtype)


def _qkprep_call(x, col0, width, w):
    b, l, _ = x.shape
    tr = _tile(l, 256, BF16_ROWS)
    assert col0 % width == 0
    off = col0 // width
    return pl.pallas_call(
        functools.partial(_headnorm_kernel, n_heads=width // HEAD_DIM),
        grid=(b, l // tr),
        in_specs=[pl.BlockSpec((None, tr, width), lambda i, j: (i, j, off)),
                  pl.BlockSpec((1, HEAD_DIM), lambda i, j: (0, 0))],
        out_specs=pl.BlockSpec((None, tr, width), lambda i, j: (i, j, 0)),
        out_shape=jax.ShapeDtypeStruct((b, l, width), BF16),
        compiler_params=_cparams("arbitrary", "arbitrary"),
        name="ctx_key_norm",
    )(x, w.reshape(1, HEAD_DIM))


def _dot_nt(a, b):
    return lax.dot_general(a, b, (((1,), (1,)), ((), ())), preferred_element_type=F32)


def _attn_kernel(sink_ref, bias_ref, q_ref, kp_ref, km_ref, kn_ref, vp_ref, vm_ref, vn_ref, kc_ref, vc_ref,
                 g_ref, o_ref, *, groups, qblocks, n_steps):
    kvh = pl.program_id(1)
    step = pl.program_id(2)
    blk = ATTN_BLOCK
    kw = jnp.concatenate([kp_ref[...], km_ref[...], kn_ref[...]], axis=0)
    vw = jnp.concatenate([vp_ref[...], vm_ref[...], vn_ref[...]], axis=0)
    kc = kc_ref[...]
    vt_c = jnp.concatenate([vc_ref[...].T, jnp.ones((BF16_ROWS, kc.shape[0]), BF16)], axis=0)
    vt_w = jnp.concatenate([vw.T, jnp.ones((BF16_ROWS, vw.shape[0]), BF16)], axis=0)
    sink2 =jnp.concatenate([jnp.full((1, blk), sink_ref[kvh * groups + g] * LOG2E, F32)
                             for g in range(groups)], axis=1)
    def scores(i):
        q = q_ref[i * blk:(i + 1) * blk, :]
        qs = jnp.concatenate([q[:, g * HEAD_DIM:(g + 1) * HEAD_DIM] for g in range(groups)], axis=0)
        first = (step == 0).astype(jnp.int32) if i == 0 else 0
        last = (step == n_steps - 1).astype(jnp.int32) if i == qblocks - 1 else 0
        bias = bias_ref[first + 2 * last]
        return _dot_nt(kc, qs), _dot_nt(kw[i * blk:(i + 3) * blk], qs) + bias

    ahead = [scores(i) for i in range(min(SCORE_LOOKAHEAD, qblocks))]
    for i in range(qblocks):
        s_c, s_w = ahead.pop(0)
        if i + SCORE_LOOKAHEAD < qblocks:
            ahead.append(scores(i + SCORE_LOOKAHEAD))
        m = jnp.maximum(jnp.maximum(jnp.max(s_c, axis=0, keepdims=True),
                                    jnp.max(s_w, axis=0, keepdims=True)), sink2)
        p_c = jnp.exp2(s_c - m).astype(BF16)
        p_w = jnp.exp2(s_w - m).astype(BF16)
        ot = (jnp.dot(vt_c, p_c, preferred_element_type=F32)
              + jnp.dot(vt_w[:, i * blk:(i + 3) * blk], p_w, preferred_element_type=F32))
        den = ot[HEAD_DIM:HEAD_DIM + 1, :] + jnp.exp2(sink2 - m)
        ot = ot[:HEAD_DIM, :] * (1.0 / den)
        o = jnp.concatenate([ot[:, g * blk:(g + 1) * blk].T for g in range(groups)], axis=1)
        gate = _silu(g_ref[i * blk:(i + 1) * blk, :].astype(F32))
        o_ref[i * blk:(i + 1) * blk, :] = (o * gate).astype(o_ref.dtype)


def _band_bias(groups):
    blk = ATTN_BLOCK
    ki = jnp.arange(3 * blk)[:, None]
    qi = jnp.arange(blk)[None, :]
    band = jnp.abs(ki - blk - qi) <= WINDOW
    out = []
    for var in range(4):
        ok = band & ((ki >= blk) | (var % 2 == 0)) & ((ki < 2 * blk) | (var // 2 == 0))
        out.append(jnp.tile(jnp.where(ok, 0.0, NEG_INF).astype(F32), (1, groups)))
    return jnp.stack(out)


def _attn_call(sink, kcn, qkvg, kvc, attn_w, kv_w):
    b, s_len, _ = qkvg.shape
    c_len = kcn.shape[1]
    n_kv = kv_w // HEAD_DIM
    groups = attn_w // kv_w
    gw = groups * HEAD_DIM
    blk = ATTN_BLOCK
    nb = s_len // blk
    qblocks = next(q for q in (8, 4, 2, 1) if nb % q == 0)
    n_steps = nb // qblocks
    k_off = attn_w // HEAD_DIM
    v_off = (attn_w + kv_w) // HEAD_DIM
    g_off = (attn_w + 2 * kv_w) // gw
    assert (attn_w + 2 * kv_w) % gw == 0

    def edge_spec(prev, off):
        if prev:
            return pl.BlockSpec((None, blk, HEAD_DIM),
                                lambda i, h, j: (i, jnp.maximum(j * qblocks - 1, 0), h + off))
        return pl.BlockSpec((None, blk, HEAD_DIM),
                            lambda i, h, j: (i, jnp.minimum((j + 1) * qblocks, nb - 1), h + off))

    def main_spec(off):
        return pl.BlockSpec((None, qblocks * blk, HEAD_DIM), lambda i, h, j: (i, j, h + off))

    def tile_spec(off):
        return pl.BlockSpec((None, qblocks * blk, gw), lambda i, h, j: (i, j, h + off))

    return pl.pallas_call(
        functools.partial(_attn_kernel, groups=groups, qblocks=qblocks, n_steps=n_steps),
        grid=(b, n_kv, n_steps),
        in_specs=[pl.BlockSpec(memory_space=pltpu.SMEM),
                  pl.BlockSpec((4, 3 * blk, groups * blk), lambda i, h, j: (0, 0, 0)),
                  tile_spec(0),
                  edge_spec(True, k_off), main_spec(k_off), edge_spec(False, k_off),
                  edge_spec(True, v_off), main_spec(v_off), edge_spec(False, v_off),
                  pl.BlockSpec((None, c_len, HEAD_DIM), lambda i, h, j: (i, 0, h)),
                  pl.BlockSpec((None, c_len, HEAD_DIM), lambda i, h, j: (i, 0, h + n_kv)),
                  tile_spec(g_off)],
        out_specs=tile_spec(0),
        out_shape=jax.ShapeDtypeStruct((b, s_len, attn_w), BF16),
        compiler_params=_cparams("arbitrary", "arbitrary", "arbitrary"),
        name="attention",
    )(sink, _band_bias(groups), qkvg, qkvg, qkvg, qkvg, qkvg, qkvg, qkvg, kcn, kvc, qkvg)


def _rope_tables(s_len):
    t = jnp.arange(s_len, dtype=jnp.int32)
    pos = jnp.stack([t // GRID_W, t % GRID_W], axis=1).astype(F32)
    half = HEAD_DIM // 2
    quarter = half // 2
    inv_freq = ROPE_BASE ** (-jnp.arange(quarter, dtype=F32) * (2.0 / half))
    ang = pos[:, :, None] * inv_freq[None, None, :]
    cos = jnp.concatenate([jnp.cos(ang), jnp.cos(ang)], axis=-1).reshape(s_len, HEAD_DIM)
    sin = jnp.concatenate([-jnp.sin(ang), jnp.sin(ang)], axis=-1).reshape(s_len, HEAD_DIM)
    return cos, sin


def kernel(x, c, ctx, c_ctx, w_mod, b_mod, norm_w, rg_w_in, rg_conv_w, rg_conv_b, rg_w_r, rg_b_r, rg_w_i,
           rg_b_i, rg_lam, rg_w_out, at_w_in, at_q_norm, at_k_norm, at_sink, at_w_out):
    b, s_len, d = x.shape
    c_len = ctx.shape[1]
    depth = w_mod.shape[0]
    assert depth == 2 and b < MOD_ROWS
    ctx_row = b

    c8 = jnp.concatenate([c, c_ctx[None, :], jnp.zeros((MOD_ROWS - b - 1, d), F32)], axis=0)
    mod = _mod_call(c8, w_mod, b_mod)
    shift, scale, gate = mod[:, :, :d], mod[:, :, d:2 * d], mod[:, :, 2 * d:]

    r = rg_w_out.shape[1]
    w_in = rg_w_in[0].astype(BF16)
    p_lat, (w_out, at_w_in_bf, at_w_out_bf) = _mm_ug_call(x, shift[0], scale[0], norm_w[0], None, w_in, r,
                                                           (rg_w_out[0], at_w_in[0], at_w_out[0]))
    p_ctx, _ = _mm_ug_call(ctx, shift[0], scale[0], norm_w[0], ctx_row, w_in, r)
    w_gate = (0.5 * jnp.concatenate([rg_w_r[0, 0], rg_w_i[0, 0], rg_w_r[0, 1], rg_w_i[0, 1]], axis=-1)).astype(BF16)
    nblk, bw = rg_w_r.shape[2], rg_w_r.shape[3]
    b_gate = 0.5 * jnp.concatenate([rg_b_r[0, 0].reshape(nblk, 1, bw), rg_b_i[0, 0].reshape(nblk, 1, bw),
                                    rg_b_r[0, 1].reshape(nblk, 1, bw), rg_b_i[0, 1].reshape(nblk, 1, bw)], axis=-1)
    z, zc = _rglru_call(p_lat, p_ctx, rg_conv_w[0], rg_conv_b[0], w_gate, b_gate, rg_lam[0])
    x = _mm_resid_call(z.reshape(b * s_len, r), w_out, x.reshape(b * s_len, d), gate[0], s_len, None)
    x = x.reshape(b, s_len, d)
    ctx = _mm_resid_call(zc.reshape(b * c_len, r), w_out, ctx.reshape(b * c_len, d), gate[0], c_len, ctx_row)
    ctx = ctx.reshape(b, c_len, d)

    attn_w = at_w_out.shape[1]
    kv_w = (at_w_in.shape[2] - 2 * attn_w) // 2
    hc = _adaln_call(ctx, shift[1], scale[1], norm_w[1], ctx_row)
    n_heads, n_kv = attn_w // HEAD_DIM, kv_w // HEAD_DIM
    w_in = at_w_in_bf
    k_norm = at_k_norm[0]
    nw = jnp.concatenate([jnp.tile(at_q_norm[0] * SCORE_SCALE, n_heads), jnp.tile(k_norm, n_kv)])
    nw = nw.reshape(1, attn_w + kv_w)
    cos, sin = _rope_tables(s_len)
    qkvg = _mm_qkvg_call(x, shift[1], scale[1], norm_w[1], w_in, cos, sin, nw).reshape(b, s_len, -1)
    kvc = _mm_call(hc.reshape(b * c_len, d), w_in, attn_w, 2 * kv_w, BF16).reshape(b, c_len, 2 * kv_w)
    kcn = _qkprep_call(kvc, 0, kv_w, k_norm)
    z = _attn_call(at_sink[0], kcn, qkvg, kvc, attn_w, kv_w)
    x = _mm_resid_call(z.reshape(b * s_len, attn_w), at_w_out_bf, x.reshape(b * s_len, d),
                       gate[1], s_len, None)
    return x.reshape(b, s_len, d)
```

```python
import functools

import jax
import jax.numpy as jnp
from jax import lax
from jax.experimental import pallas as pl
from jax.experimental.pallas import tpu as pltpu

F32 = jnp.float32
BF16 = jnp.bfloat16

HEAD_DIM = 128
WINDOW = 128
ATTN_BLOCK = 128
GRID_W = 64
ROPE_BASE = 10000.0
NORM_EPS = 1e-6
NEG_INF = -1e30
LRU_C = 8.0
CONV_W = 4
CONV_PAD_LO = 2
LOG2E = 1.4426950408889634
LN2 = 0.6931471805599453
SCORE_SCALE = HEAD_DIM ** -0.5 * LOG2E
SQRT2 = 1.4142135623730951
F32_TINY = 1.1754943508222875e-38

SUBLANES = 8
LANES = 128
BF16_ROWS = 16
MOD_ROWS = 8
VMEM_LIMIT = 56 * 1024 * 1024
SCAN_STEPS = 64
NORM_ROW_BLOCKS = 4
ADALN_SEGS = 2
ROW_TILE_GROUP = 2
SCORE_LOOKAHEAD = 2


def _cparams(*sem):
    return pltpu.CompilerParams(dimension_semantics=sem, vmem_limit_bytes=VMEM_LIMIT)


def _tile(n, pref, unit):
    if n <= pref:
        return n
    t = (pref // unit) * unit
    while n % t:
        t -= unit
    return t


def _sigmoid(x):
    return 1.0 / (1.0 + jnp.exp(-x))


def _silu(x):
    return x * _sigmoid(x)


def _mod_kernel(c_ref, w_ref, b_ref, o_ref):
    s = _silu(c_ref[...]).astype(BF16)
    o_ref[...] = jnp.dot(s, w_ref[...].astype(BF16), preferred_element_type=F32) + b_ref[...]


def _mod_call(c8, w_mod, b_mod):
    depth, d, n3 = w_mod.shape
    tn = _tile(n3, 768, LANES)
    return pl.pallas_call(
        _mod_kernel,
        grid=(depth, n3 // tn),
        in_specs=[pl.BlockSpec((MOD_ROWS, d), lambda l, j: (0, 0)),
                  pl.BlockSpec((None, d, tn), lambda l, j: (l, 0, j)),
                  pl.BlockSpec((None, 1, tn), lambda l, j: (l, 0, j))],
        out_specs=pl.BlockSpec((None, MOD_ROWS, tn), lambda l, j: (l, 0, j)),
        out_shape=jax.ShapeDtypeStruct((depth, MOD_ROWS, n3), F32),
        compiler_params=_cparams("arbitrary", "arbitrary"),
        name="mod",
    )(c8, w_mod, b_mod.reshape(depth, 1, n3))


def _adaln_kernel(x_ref, sh_ref, sc_ref, nw_ref, o_ref, *, row):
    r = pl.program_id(0) if row is None else row
    x = x_ref[...]
    inv = lax.rsqrt(jnp.mean(x * x, axis=-1, keepdims=True) + NORM_EPS)
    xn = (x * inv) * nw_ref[...]
    o_ref[...] = (xn * (1.0 + sc_ref[pl.ds(r, 1), :]) + sh_ref[pl.ds(r, 1), :]).astype(o_ref.dtype)


def _adaln_call(x, shift, scale, nw, row):
    b, l, d = x.shape
    tr = _tile(l, 512, BF16_ROWS)
    return pl.pallas_call(
        functools.partial(_adaln_kernel, row=row),
        grid=(b, l // tr),
        in_specs=[pl.BlockSpec((None, tr, d), lambda i, j: (i, j, 0)),
                  pl.BlockSpec((MOD_ROWS, d), lambda i, j: (0, 0)),
                  pl.BlockSpec((MOD_ROWS, d), lambda i, j: (0, 0)),
                  pl.BlockSpec((1, d), lambda i, j: (0, 0))],
        out_specs=pl.BlockSpec((None, tr, d), lambda i, j: (i, j, 0)),
        out_shape=jax.ShapeDtypeStruct((b, l, d), BF16),
        compiler_params=_cparams("arbitrary", "arbitrary"),
        name="adaln",
    )(x, shift, scale, nw.reshape(1, d))


def _mm_kernel(a_ref, w_ref, o_ref):
    o_ref[...] = jnp.dot(a_ref[...], w_ref[...], preferred_element_type=F32).astype(o_ref.dtype)


def _mm_call(a, w, col0, ncols, out_dtype):
    m, k = a.shape
    tm = _tile(m, 1024, BF16_ROWS)
    tn = _tile(ncols, 512, LANES)
    assert col0 % tn == 0
    off = col0 // tn
    return pl.pallas_call(
        _mm_kernel,
        grid=(m // tm, ncols // tn),
        in_specs=[pl.BlockSpec((tm, k), lambda i, j: (i, 0)),
                  pl.BlockSpec((k, tn), lambda i, j: (0, j + off))],
        out_specs=pl.BlockSpec((tm, tn), lambda i, j: (i, j)),
        out_shape=jax.ShapeDtypeStruct((m, ncols), out_dtype),
        compiler_params=_cparams("arbitrary", "arbitrary"),
        name="matmul",
    )(a, w)


def _adaln_rows(x, sh_ref, sc_ref, nwk_ref, r):
    inv = lax.rsqrt(jnp.mean(x * x, axis=-1, keepdims=True) + NORM_EPS)
    xn = (x * inv) * nwk_ref[...]
    return (xn * (1.0 + sc_ref[pl.ds(r, 1), :]) + sh_ref[pl.ds(r, 1), :]).astype(BF16)


def _mm_qkvg_kernel(x_ref, sh_ref, sc_ref, nwk_ref, w_ref, cos_ref, sin_ref, nw_ref, o_ref, h_ref, *,
                    n_norm_tiles, tiles_per_seq):
    j = pl.program_id(1)
    slot = pl.program_id(2)
    tile = pl.program_id(0) * h_ref.shape[0] + slot
    tm = x_ref.shape[0]
    rb = tm // NORM_ROW_BLOCKS if tm % (NORM_ROW_BLOCKS * BF16_ROWS) == 0 else tm
    quarter = HEAD_DIM // 4

    def qk_rows(r0, a):
        y = jnp.dot(a, w_ref[...], preferred_element_type=F32)
        cos = cos_ref[r0:r0 + rb, :]
        sin = sin_ref[r0:r0 + rb, :]
        lane = lax.broadcasted_iota(jnp.int32, (rb, HEAD_DIM), 1)
        x1_lanes = (lane % (2 * quarter)) < quarter
        for h in range(y.shape[1] // HEAD_DIM):
            sl = slice(h * HEAD_DIM, (h + 1) * HEAD_DIM)
            x = y[:, sl]
            inv = lax.rsqrt(jnp.mean(x * x, axis=-1, keepdims=True) + NORM_EPS)
            xn = (x * inv) * nw_ref[:, sl]
            partner = jnp.where(x1_lanes, pltpu.roll(xn, HEAD_DIM - quarter, 1), pltpu.roll(xn, quarter, 1))
            o_ref[r0:r0 + rb, sl] = (xn * cos + partner * sin).astype(o_ref.dtype)

    @pl.when(j == 0)
    def _():
        r = tile // tiles_per_seq
        for r0 in range(0, tm, rb):
            a = _adaln_rows(x_ref[r0:r0 + rb, :], sh_ref, sc_ref, nwk_ref, r)
            h_ref[slot, r0:r0 + rb, :] = a
            qk_rows(r0, a)

    @pl.when((j > 0) & (j < n_norm_tiles))
    def _():
        for r0 in range(0, tm, rb):
            qk_rows(r0, h_ref[slot, r0:r0 + rb, :])

    @pl.when(j >= n_norm_tiles)
    def _():
        o_ref[...] = jnp.dot(h_ref[slot], w_ref[...], preferred_element_type=F32).astype(o_ref.dtype)


def _mm_qkvg_call(x, shift, scale, nwk, w, cos, sin, nw):
    b, s_len, k = x.shape
    m = b * s_len
    n = w.shape[1]
    tm = _tile(s_len, 512, BF16_ROWS)
    tn = _tile(nw.shape[1], 1024, HEAD_DIM)
    assert n % tn == 0 and nw.shape[1] >= tn
    n_norm_tiles = nw.shape[1] // tn
    t_per_seq = s_len // tm
    grp = ROW_TILE_GROUP if (m // tm) % ROW_TILE_GROUP == 0 else 1

    def x_map(i, j, s):
        return (i * grp + jnp.where(j == 0, s, grp - 1), 0)

    def pos_map(i, j, s):
        return ((i * grp + s) % t_per_seq, 0)

    return pl.pallas_call(
        functools.partial(_mm_qkvg_kernel, n_norm_tiles=n_norm_tiles, tiles_per_seq=t_per_seq),
        grid=(m // (tm * grp), n // tn, grp),
        in_specs=[pl.BlockSpec((tm, k), x_map),
                  pl.BlockSpec((MOD_ROWS, k), lambda i, j, s: (0, 0)),
                  pl.BlockSpec((MOD_ROWS, k), lambda i, j, s: (0, 0)),
                  pl.BlockSpec((1, k), lambda i, j, s: (0, 0)),
                  pl.BlockSpec((k, tn), lambda i, j, s: (0, j)),
                  pl.BlockSpec((tm, HEAD_DIM), pos_map),
                  pl.BlockSpec((tm, HEAD_DIM), pos_map),
                  pl.BlockSpec((1, tn), lambda i, j, s: (0, jnp.minimum(j, n_norm_tiles - 1)))],
        out_specs=pl.BlockSpec((tm, tn), lambda i, j, s: (i * grp + s, j)),
        out_shape=jax.ShapeDtypeStruct((m, n), BF16),
        scratch_shapes=[pltpu.VMEM((grp, tm, k), BF16)],
        compiler_params=_cparams("arbitrary", "arbitrary", "arbitrary"),
        name="matmul_qkvg",
    )(x.reshape(m, k), shift, scale, nwk.reshape(1, k), w, cos, sin, nw)


def _mm_ug_kernel(x_ref, sh_ref, sc_ref, nwk_ref, wu_ref, wg_ref, *rest, tmm, n_side, row, n_m):
    side_in, o_ref, side_out, h_ref = rest[:n_side], rest[n_side], rest[n_side + 1:-1], rest[-1]
    nbat, _, _, k = x_ref.shape
    j = pl.program_id(1)
    slot = pl.program_id(2)
    tile = pl.program_id(0) * h_ref.shape[0] + slot
    for src, dst in zip(side_in, side_out):
        dst[...] = src[...].astype(dst.dtype)

    def project(bi, seg0, nseg, a):
        u = jnp.dot(a, wu_ref[...], preferred_element_type=F32).astype(BF16).astype(F32)
        g = jnp.dot(a, wg_ref[...], preferred_element_type=F32).astype(BF16).astype(F32)
        word = lax.bitcast_convert_type(u, jnp.uint32) | (lax.bitcast_convert_type(g, jnp.uint32) >> 16)
        for ds in range(nseg):
            for s in range(o_ref.shape[1]):
                o_ref[bi, s, pl.ds(seg0 + ds, tmm, stride=SUBLANES), :] = (
                    word[ds * tmm:(ds + 1) * tmm, s * LANES:(s + 1) * LANES])

    @pl.when(j == 0)
    def _():
        for bi in range(nbat):
            r = ((tile // n_m) * nbat + bi) if row is None else row
            for seg0 in range(0, SUBLANES, ADALN_SEGS):
                x = x_ref[bi, seg0:seg0 + ADALN_SEGS].reshape(ADALN_SEGS * tmm, k)
                a = _adaln_rows(x, sh_ref, sc_ref, nwk_ref, r)
                h_ref[slot, bi, seg0:seg0 + ADALN_SEGS] = a.reshape(ADALN_SEGS, tmm, k)
                project(bi, seg0, ADALN_SEGS, a)

    @pl.when(j > 0)
    def _():
        for bi in range(nbat):
            project(bi, 0, SUBLANES, h_ref[slot, bi].reshape(SUBLANES * tmm, k))


def _side_rows(rows, steps):
    cand = BF16_ROWS
    while rows % cand or rows // cand > steps:
        cand += BF16_ROWS
    return cand


def _mm_ug_call(x, shift, scale, nwk, row, w, r, side=()):
    b, l, k = x.shape
    lseg = l // SUBLANES
    tmm = _tile(lseg, 64, BF16_ROWS)
    nbat = _tile(b, max(1, 64 // tmm), 1)
    tn = _tile(r, 512, LANES)
    nslab = tn // LANES
    n_m = lseg // tmm
    n_tiles = (b // nbat) * n_m
    grp = ROW_TILE_GROUP if n_tiles % ROW_TILE_GROUP == 0 else 1
    grid = (n_tiles // grp, r // tn, grp)
    steps = grid[0] * grid[1] * grid[2]

    def x_map(i, j, s):
        t = i * grp + jnp.where(j == 0, s, grp - 1)
        return (t // n_m, 0, t % n_m, 0)

    def out_map(i, j, s):
        t = i * grp + s
        return (t // n_m, j, t % n_m, 0)

    side_specs, side_shapes = [], []
    for t in side:
        rows = _side_rows(t.shape[0], steps)
        last = t.shape[0] // rows - 1
        spec = pl.BlockSpec((rows, t.shape[1]),
                            lambda i, j, s, last=last: (jnp.minimum((i * grid[1] + j) * grp + s, last), 0))
        side_specs.append(spec)
        side_shapes.append(jax.ShapeDtypeStruct(t.shape, BF16))
    out = pl.pallas_call(
        functools.partial(_mm_ug_kernel, tmm=tmm, n_side=len(side), row=row, n_m=n_m),
        grid=grid,
        in_specs=[pl.BlockSpec((nbat, SUBLANES, tmm, k), x_map),
                  pl.BlockSpec((MOD_ROWS, k), lambda i, j, s: (0, 0)),
                  pl.BlockSpec((MOD_ROWS, k), lambda i, j, s: (0, 0)),
                  pl.BlockSpec((1, k), lambda i, j, s: (0, 0)),
                  pl.BlockSpec((k, tn), lambda i, j, s: (0, j)),
                  pl.BlockSpec((k, tn), lambda i, j, s: (0, j + r // tn))] + side_specs,
        out_specs=[pl.BlockSpec((nbat, nslab, SUBLANES * tmm, LANES), out_map)] + side_specs,
        out_shape=[jax.ShapeDtypeStruct((b, r // LANES, l, LANES), jnp.uint32)] + side_shapes,
        scratch_shapes=[pltpu.VMEM((grp, nbat, SUBLANES, tmm, k), BF16)],
        compiler_params=_cparams("arbitrary", "arbitrary", "arbitrary"),
        name="matmul_ug",
    )(x.reshape(b, SUBLANES, lseg, k), shift, scale, nwk.reshape(1, k), w, w, *side)
    return out[0], out[1:]


def _mm_resid_kernel(a_ref, w_ref, x_ref, g_ref, o_ref, *, row, tiles_per_batch):
    r = (pl.program_id(0) // tiles_per_batch) if row is None else row
    y = jnp.dot(a_ref[...], w_ref[...], preferred_element_type=F32)
    o_ref[...] = x_ref[...] + g_ref[pl.ds(r, 1), :] * y


def _mm_resid_call(a, w, x, gate, rows_per_batch, row):
    m, k = a.shape
    n = w.shape[1]
    tm = _tile(rows_per_batch, 1024, BF16_ROWS)
    tn = _tile(n, 512, LANES)
    return pl.pallas_call(
        functools.partial(_mm_resid_kernel, row=row, tiles_per_batch=rows_per_batch // tm),
        grid=(m // tm, n // tn),
        in_specs=[pl.BlockSpec((tm, k), lambda i, j: (i, 0)),
                  pl.BlockSpec((k, tn), lambda i, j: (0, j)),
                  pl.BlockSpec((tm, tn), lambda i, j: (i, j)),
                  pl.BlockSpec((MOD_ROWS, tn), lambda i, j: (0, j))],
        out_specs=pl.BlockSpec((tm, tn), lambda i, j: (i, j)),
        out_shape=jax.ShapeDtypeStruct((m, n), F32),
        compiler_params=_cparams("arbitrary", "arbitrary"),
        name="matmul_resid",
    )(a, w, x, gate)


def _rglru_kernel(pl_ref, pc_ref, cw_ref, cb_ref, wg_ref, bg_ref, lam_ref, zl_ref, zc_ref,
                  cu_ref, hl_ref, pp_ref, zs_ref, *, s_len, c_len, bw):
    nslab = bw // LANES
    cw = cw_ref[...]
    cb = cb_ref[...]
    nlam = -lam_ref[...]
    softplus = jnp.maximum(nlam, 0.0) + jnp.log1p(jnp.exp(-jnp.abs(nlam)))
    k2 = (-0.5 * LRU_C * LOG2E) * softplus
    sub = lax.broadcasted_iota(jnp.int32, (SUBLANES, bw), 0)

    def unpack(p_ref, r0, n, low):
        parts = []
        for s in range(nslab):
            w = p_ref[s, pl.ds(r0, n), :]
            w = (w << 16) if low else (w & jnp.uint32(0xFFFF0000))
            parts.append(lax.bitcast_convert_type(w, F32))
        return jnp.concatenate(parts, axis=1)

    def seg_down(v):
        return jnp.where(sub == 0, 0.0, pltpu.roll(v, 1, 0))

    def seg_up(v):
        return jnp.where(sub == SUBLANES - 1, 0.0, pltpu.roll(v, SUBLANES - 1, 0))

    def conv_chunk(p_ref, k, n, n_chunks):
        length = n * n_chunks
        r0 = pl.multiple_of(k * n, n)
        lo = CONV_PAD_LO * SUBLANES
        hi = (CONV_W - 1 - CONV_PAD_LO) * SUBLANES
        cur = unpack(p_ref, r0, n, False)
        p0 = pl.multiple_of(jnp.where(k == 0, length - lo, r0 - lo), SUBLANES)
        n0 = pl.multiple_of(jnp.where(k == n_chunks - 1, 0, r0 + n), SUBLANES)
        prev = unpack(p_ref, p0, lo, False)
        nxt = unpack(p_ref, n0, hi, False)
        prev_edge = jnp.concatenate([seg_down(prev[i * SUBLANES:(i + 1) * SUBLANES])
                                     for i in range(CONV_PAD_LO)], axis=0)
        nxt_edge = jnp.concatenate([seg_up(nxt[i * SUBLANES:(i + 1) * SUBLANES])
                                    for i in range(CONV_W - 1 - CONV_PAD_LO)], axis=0)
        prev = jnp.where(k == 0, prev_edge, prev)
        nxt = jnp.where(k == n_chunks - 1, nxt_edge, nxt)
        ext = jnp.concatenate([prev, cur, nxt], axis=0)
        acc = cb + cw[0:1] * ext[0:n]
        for t in range(1, CONV_W):
            acc = acc + cw[t:t + 1] * ext[t * SUBLANES:t * SUBLANES + n]
        return acc

    def coeffs(uc, d):
        lo, hi = 2 * d * bw, 2 * (d + 1) * bw
        zg = jnp.dot(uc.astype(BF16), wg_ref[:, lo:hi], preferred_element_type=F32) + bg_ref[:, lo:hi]
        tr = jnp.tanh(zg[:, :bw])
        ti = jnp.tanh(zg[:, bw:])
        kd = k2[d:d + 1]
        la2 = kd * tr + kd
        a = jnp.exp2(la2)
        th = jnp.tanh(la2 * (-LN2))
        root = th * lax.rsqrt(jnp.maximum(th * th + th, F32_TINY))
        return a, root * ((ti + 1.0) * uc)

    def run_sequence(p_ref, z_ref, length, h0f, h0b):
        lseg = length // SUBLANES
        steps = _tile(lseg, SCAN_STEPS, BF16_ROWS)
        n = steps * SUBLANES
        n_chunks = length // n

        def conv_body(k, carry):
            cu_ref[pl.ds(pl.multiple_of(k * n, n), n), :] = conv_chunk(p_ref, k, n, n_chunks)
            return carry

        lax.fori_loop(0, n_chunks, conv_body, 0)

        def chunk_coeffs(k):
            return (coeffs(cu_ref[pl.ds(k * n, n), :], 0)
                    + coeffs(cu_ref[pl.ds((n_chunks - 1 - k) * n, n), :], 1))

        def chunk_scan(k, co, carry):
            af, bf, ab, bb = co
            hf, pf, hb, pb = carry
            rf = k * n
            rb = (n_chunks - 1 - k) * n
            hs_f, ps_f, hs_b, ps_b = [None] * steps, [None] * steps, [None] * steps, [None] * steps
            for m in range(steps):
                sf = slice(m * SUBLANES, (m + 1) * SUBLANES)
                hf = af[sf] * hf + bf[sf]
                pf = af[sf] * pf
                hs_f[m], ps_f[m] = hf, pf
                mb = steps - 1 - m
                sb = slice(mb * SUBLANES, (mb + 1) * SUBLANES)
                hb = ab[sb] * hb + bb[sb]
                pb = ab[sb] * pb
                hs_b[mb], ps_b[mb] = hb, pb
            hl_ref[0, pl.ds(rf, n), :] = jnp.concatenate(hs_f, axis=0)
            pp_ref[0, pl.ds(rf, n), :] = jnp.concatenate(ps_f, axis=0)
            hl_ref[1, pl.ds(rb, n), :] = jnp.concatenate(hs_b, axis=0)
            pp_ref[1, pl.ds(rb, n), :] = jnp.concatenate(ps_b, axis=0)
            return hf, pf, hb, pb

        carry = (jnp.zeros((SUBLANES, bw), F32), jnp.ones((SUBLANES, bw), F32),
                 jnp.zeros((SUBLANES, bw), F32), jnp.ones((SUBLANES, bw), F32))
        nxt = chunk_coeffs(0)
        for k in range(n_chunks):
            cur = nxt
            if k + 1 < n_chunks:
                nxt = chunk_coeffs(k + 1)
            carry = chunk_scan(k, cur, carry)
        ef, qf, eb, qb = carry

        rows_f, rows_b = [None] * SUBLANES, [None] * SUBLANES
        c = h0f
        for s in range(SUBLANES):
            rows_f[s] = c
            c = qf[s:s + 1] * c + ef[s:s + 1]
        hf_end = c
        c = h0b
        for s in range(SUBLANES - 1, -1, -1):
            rows_b[s] = c
            c = qb[s:s + 1] * c + eb[s:s + 1]
        hb_end = c
        cf = jnp.concatenate(rows_f, axis=0)[None]
        cbk = jnp.concatenate(rows_b, axis=0)[None]

        def out_body(k, carry):
            r0 = pl.multiple_of(k * n, n)
            sl = pl.ds(r0, n)
            hf = hl_ref[0, sl, :].reshape(steps, SUBLANES, bw) + pp_ref[0, sl, :].reshape(steps, SUBLANES, bw) * cf
            hb = hl_ref[1, sl, :].reshape(steps, SUBLANES, bw) + pp_ref[1, sl, :].reshape(steps, SUBLANES, bw) * cbk
            g = unpack(p_ref, r0, n, True)
            gate = (g * (0.25 * SQRT2)) * (jnp.tanh(g * 0.5) + 1.0)
            z = (hf + hb).reshape(n, bw) * gate
            for s in range(nslab):
                zs_ref[s, 0:n, :] = z[:, s * LANES:(s + 1) * LANES]
            for seg in range(SUBLANES):
                piece = jnp.concatenate([zs_ref[s, pl.ds(seg, steps, stride=SUBLANES), :]
                                         for s in range(nslab)], axis=1)
                t0 = pl.multiple_of(seg * lseg + k * steps, steps)
                z_ref[pl.ds(t0, steps), :] = piece.astype(z_ref.dtype)
            return carry

        lax.fori_loop(0, n_chunks, out_body, 0)
        return hf_end, hb_end

    zero = jnp.zeros((1, bw), F32)
    hf0, hb0 = run_sequence(pc_ref, zc_ref, c_len, zero, zero)
    run_sequence(pl_ref, zl_ref, s_len, hf0, hb0)


def _rglru_call(p_lat, p_ctx, conv_w, conv_b, w_gate, b_gate, lam):
    b, nsl, s_len, _ = p_lat.shape
    c_len = p_ctx.shape[2]
    r = nsl * LANES
    nblk, bw, _ = w_gate.shape
    nslab = bw // LANES
    lmax = max(s_len, c_len)
    chunk = SUBLANES * SCAN_STEPS
    kern = functools.partial(_rglru_kernel, s_len=s_len, c_len=c_len, bw=bw)
    return pl.pallas_call(
        kern,
        grid=(b, nblk),
        in_specs=[pl.BlockSpec((None, nslab, s_len, LANES), lambda i, j: (i, j, 0, 0)),
                  pl.BlockSpec((None, nslab, c_len, LANES), lambda i, j: (i, j, 0, 0)),
                  pl.BlockSpec((CONV_W, bw), lambda i, j: (0, j)),
                  pl.BlockSpec((1, bw), lambda i, j: (0, j)),
                  pl.BlockSpec((None, bw, 4 * bw), lambda i, j: (j, 0, 0)),
                  pl.BlockSpec((None, 1, 4 * bw), lambda i, j: (j, 0, 0)),
                  pl.BlockSpec((2, bw), lambda i, j: (0, j))],
        out_specs=[pl.BlockSpec((None, s_len, bw), lambda i, j: (i, 0, j)),
                   pl.BlockSpec((None, c_len, bw), lambda i, j: (i, 0, j))],
        out_shape=[jax.ShapeDtypeStruct((b, s_len, r), BF16),
                   jax.ShapeDtypeStruct((b, c_len, r), BF16)],
        scratch_shapes=[pltpu.VMEM((lmax, bw), F32),
                        pltpu.VMEM((2, lmax, bw), F32),
                        pltpu.VMEM((2, lmax, bw), F32),
                        pltpu.VMEM((nslab, chunk, LANES), F32)],
        compiler_params=_cparams("arbitrary", "arbitrary"),
        name="rglru",
    )(p_lat, p_ctx, conv_w, conv_b.reshape(1, r), w_gate, b_gate, lam)


def _headnorm_kernel(x_ref, w_ref, o_ref, *, n_heads):
    w = w_ref[...]
    for h in range(n_heads):
        sl = slice(h * HEAD_DIM, (h + 1) * HEAD_DIM)
        x = x_ref[:, sl].astype(F32)
        inv = lax.rsqrt(jnp.mean(x * x, axis=-1, keepdims=True) + NORM_EPS)
        o_ref[:, sl] = ((x * inv) * w).astype(o_ref.dtype)


def _qkprep_call(x, col0, width, w):
    b, l, _ = x.shape
    tr = _tile(l, 256, BF16_ROWS)
    assert col0 % width == 0
    off = col0 // width
    return pl.pallas_call(
        functools.partial(_headnorm_kernel, n_heads=width // HEAD_DIM),
        grid=(b, l // tr),
        in_specs=[pl.BlockSpec((None, tr, width), lambda i, j: (i, j, off)),
                  pl.BlockSpec((1, HEAD_DIM), lambda i, j: (0, 0))],
        out_specs=pl.BlockSpec((None, tr, width), lambda i, j: (i, j, 0)),
        out_shape=jax.ShapeDtypeStruct((b, l, width), BF16),
        compiler_params=_cparams("arbitrary", "arbitrary"),
        name="ctx_key_norm",
    )(x, w.reshape(1, HEAD_DIM))


def _dot_nt(a, b):
    return lax.dot_general(a, b, (((1,), (1,)), ((), ())), preferred_element_type=F32)


def _attn_kernel(sink_ref, bias_ref, q_ref, kp_ref, km_ref, kn_ref, vp_ref, vm_ref, vn_ref, kc_ref, vc_ref,
                 g_ref, o_ref, *, groups, qblocks, n_steps):
    kvh = pl.program_id(1)
    step = pl.program_id(2)
    blk = ATTN_BLOCK
    kw = jnp.concatenate([kp_ref[...], km_ref[...], kn_ref[...]], axis=0)
    vw = jnp.concatenate([vp_ref[...], vm_ref[...], vn_ref[...]], axis=0)
    kc = kc_ref[...]
    vt_c = jnp.concatenate([vc_ref[...].T, jnp.ones((BF16_ROWS, kc.shape[0]), BF16)], axis=0)
    vt_w = jnp.concatenate([vw.T, jnp.ones((BF16_ROWS, vw.shape[0]), BF16)], axis=0)
    sink2 =jnp.concatenate([jnp.full((1, blk), sink_ref[kvh * groups + g] * LOG2E, F32)
                             for g in range(groups)], axis=1)
    def scores(i):
        q = q_ref[i * blk:(i + 1) * blk, :]
        qs = jnp.concatenate([q[:, g * HEAD_DIM:(g + 1) * HEAD_DIM] for g in range(groups)], axis=0)
        first = (step == 0).astype(jnp.int32) if i == 0 else 0
        last = (step == n_steps - 1).astype(jnp.int32) if i == qblocks - 1 else 0
        bias = bias_ref[first + 2 * last]
        return _dot_nt(kc, qs), _dot_nt(kw[i * blk:(i + 3) * blk], qs) + bias

    ahead = [scores(i) for i in range(min(SCORE_LOOKAHEAD, qblocks))]
    for i in range(qblocks):
        s_c, s_w = ahead.pop(0)
        if i + SCORE_LOOKAHEAD < qblocks:
            ahead.append(scores(i + SCORE_LOOKAHEAD))
        m = jnp.maximum(jnp.maximum(jnp.max(s_c, axis=0, keepdims=True),
                                    jnp.max(s_w, axis=0, keepdims=True)), sink2)
        p_c = jnp.exp2(s_c - m).astype(BF16)
        p_w = jnp.exp2(s_w - m).astype(BF16)
        ot = (jnp.dot(vt_c, p_c, preferred_element_type=F32)
              + jnp.dot(vt_w[:, i * blk:(i + 3) * blk], p_w, preferred_element_type=F32))
        den = ot[HEAD_DIM:HEAD_DIM + 1, :] + jnp.exp2(sink2 - m)
        ot = ot[:HEAD_DIM, :] * (1.0 / den)
        o = jnp.concatenate([ot[:, g * blk:(g + 1) * blk].T for g in range(groups)], axis=1)
        gate = _silu(g_ref[i * blk:(i + 1) * blk, :].astype(F32))
        o_ref[i * blk:(i + 1) * blk, :] = (o * gate).astype(o_ref.dtype)


def _band_bias(groups):
    blk = ATTN_BLOCK
    ki = jnp.arange(3 * blk)[:, None]
    qi = jnp.arange(blk)[None, :]
    band = jnp.abs(ki - blk - qi) <= WINDOW
    out = []
    for var in range(4):
        ok = band & ((ki >= blk) | (var % 2 == 0)) & ((ki < 2 * blk) | (var // 2 == 0))
        out.append(jnp.tile(jnp.where(ok, 0.0, NEG_INF).astype(F32), (1, groups)))
    return jnp.stack(out)


def _attn_call(sink, kcn, qkvg, kvc, attn_w, kv_w):
    b, s_len, _ = qkvg.shape
    c_len = kcn.shape[1]
    n_kv = kv_w // HEAD_DIM
    groups = attn_w // kv_w
    gw = groups * HEAD_DIM
    blk = ATTN_BLOCK
    nb = s_len // blk
    qblocks = next(q for q in (8, 4, 2, 1) if nb % q == 0)
    n_steps = nb // qblocks
    k_off = attn_w // HEAD_DIM
    v_off = (attn_w + kv_w) // HEAD_DIM
    g_off = (attn_w + 2 * kv_w) // gw
    assert (attn_w + 2 * kv_w) % gw == 0

    def edge_spec(prev, off):
        if prev:
            return pl.BlockSpec((None, blk, HEAD_DIM),
                                lambda i, h, j: (i, jnp.maximum(j * qblocks - 1, 0), h + off))
        return pl.BlockSpec((None, blk, HEAD_DIM),
                            lambda i, h, j: (i, jnp.minimum((j + 1) * qblocks, nb - 1), h + off))

    def main_spec(off):
        return pl.BlockSpec((None, qblocks * blk, HEAD_DIM), lambda i, h, j: (i, j, h + off))

    def tile_spec(off):
        return pl.BlockSpec((None, qblocks * blk, gw), lambda i, h, j: (i, j, h + off))

    return pl.pallas_call(
        functools.partial(_attn_kernel, groups=groups, qblocks=qblocks, n_steps=n_steps),
        grid=(b, n_kv, n_steps),
        in_specs=[pl.BlockSpec(memory_space=pltpu.SMEM),
                  pl.BlockSpec((4, 3 * blk, groups * blk), lambda i, h, j: (0, 0, 0)),
                  tile_spec(0),
                  edge_spec(True, k_off), main_spec(k_off), edge_spec(False, k_off),
                  edge_spec(True, v_off), main_spec(v_off), edge_spec(False, v_off),
                  pl.BlockSpec((None, c_len, HEAD_DIM), lambda i, h, j: (i, 0, h)),
                  pl.BlockSpec((None, c_len, HEAD_DIM), lambda i, h, j: (i, 0, h + n_kv)),
                  tile_spec(g_off)],
        out_specs=tile_spec(0),
        out_shape=jax.ShapeDtypeStruct((b, s_len, attn_w), BF16),
        compiler_params=_cparams("arbitrary", "arbitrary", "arbitrary"),
        name="attention",
    )(sink, _band_bias(groups), qkvg, qkvg, qkvg, qkvg, qkvg, qkvg, qkvg, kcn, kvc, qkvg)


def _rope_tables(s_len):
    t = jnp.arange(s_len, dtype=jnp.int32)
    pos = jnp.stack([t // GRID_W, t % GRID_W], axis=1).astype(F32)
    half = HEAD_DIM // 2
    quarter = half // 2
    inv_freq = ROPE_BASE ** (-jnp.arange(quarter, dtype=F32) * (2.0 / half))
    ang = pos[:, :, None] * inv_freq[None, None, :]
    cos = jnp.concatenate([jnp.cos(ang), jnp.cos(ang)], axis=-1).reshape(s_len, HEAD_DIM)
    sin = jnp.concatenate([-jnp.sin(ang), jnp.sin(ang)], axis=-1).reshape(s_len, HEAD_DIM)
    return cos, sin


def kernel(x, c, ctx, c_ctx, w_mod, b_mod, norm_w, rg_w_in, rg_conv_w, rg_conv_b, rg_w_r, rg_b_r, rg_w_i,
           rg_b_i, rg_lam, rg_w_out, at_w_in, at_q_norm, at_k_norm, at_sink, at_w_out):
    b, s_len, d = x.shape
    c_len = ctx.shape[1]
    depth = w_mod.shape[0]
    assert depth == 2 and b < MOD_ROWS
    ctx_row = b

    c8 = jnp.concatenate([c, c_ctx[None, :], jnp.zeros((MOD_ROWS - b - 1, d), F32)], axis=0)
    mod = _mod_call(c8, w_mod, b_mod)
    shift, scale, gate = mod[:, :, :d], mod[:, :, d:2 * d], mod[:, :, 2 * d:]

    r = rg_w_out.shape[1]
    w_in = rg_w_in[0].astype(BF16)
    p_lat, (w_out, at_w_in_bf, at_w_out_bf) = _mm_ug_call(x, shift[0], scale[0], norm_w[0], None, w_in, r,
                                                           (rg_w_out[0], at_w_in[0], at_w_out[0]))
    p_ctx, _ = _mm_ug_call(ctx, shift[0], scale[0], norm_w[0], ctx_row, w_in, r)
    w_gate = (0.5 * jnp.concatenate([rg_w_r[0, 0], rg_w_i[0, 0], rg_w_r[0, 1], rg_w_i[0, 1]], axis=-1)).astype(BF16)
    nblk, bw = rg_w_r.shape[2], rg_w_r.shape[3]
    b_gate = 0.5 * jnp.concatenate([rg_b_r[0, 0].reshape(nblk, 1, bw), rg_b_i[0, 0].reshape(nblk, 1, bw),
                                    rg_b_r[0, 1].reshape(nblk, 1, bw), rg_b_i[0, 1].reshape(nblk, 1, bw)], axis=-1)
    z, zc = _rglru_call(p_lat, p_ctx, rg_conv_w[0], rg_conv_b[0], w_gate, b_gate, rg_lam[0])
    x = _mm_resid_call(z.reshape(b * s_len, r), w_out, x.reshape(b * s_len, d), gate[0], s_len, None)
    x = x.reshape(b, s_len, d)
    ctx = _mm_resid_call(zc.reshape(b * c_len, r), w_out, ctx.reshape(b * c_len, d), gate[0], c_len, ctx_row)
    ctx = ctx.reshape(b, c_len, d)

    attn_w = at_w_out.shape[1]
    kv_w = (at_w_in.shape[2] - 2 * attn_w) // 2
    hc = _adaln_call(ctx, shift[1], scale[1], norm_w[1], ctx_row)
    n_heads, n_kv = attn_w // HEAD_DIM, kv_w // HEAD_DIM
    w_in = at_w_in_bf
    k_norm = at_k_norm[0]
    nw = jnp.concatenate([jnp.tile(at_q_norm[0] * SCORE_SCALE, n_heads), jnp.tile(k_norm, n_kv)])
    nw = nw.reshape(1, attn_w + kv_w)
    cos, sin = _rope_tables(s_len)
    qkvg = _mm_qkvg_call(x, shift[1], scale[1], norm_w[1], w_in, cos, sin, nw).reshape(b, s_len, -1)
    kvc = _mm_call(hc.reshape(b * c_len, d), w_in, attn_w, 2 * kv_w, BF16).reshape(b, c_len, 2 * kv_w)
    kcn = _qkprep_call(kvc, 0, kv_w, k_norm)
    z = _attn_call(at_sink[0], kcn, qkvg, kvc, attn_w, kv_w)
    x = _mm_resid_call(z.reshape(b * s_len, attn_w), at_w_out_bf, x.reshape(b * s_len, d),
                       gate[1], s_len, None)
    return x.reshape(b, s_len, d)
```

```python
import functools

import jax
import jax.numpy as jnp
from jax import lax
from jax.experimental import pallas as pl
from jax.experimental.pallas import tpu as pltpu

F32 = jnp.float32
BF16 = jnp.bfloat16

HEAD_DIM = 128
WINDOW = 128
ATTN_BLOCK = 128
GRID_W = 64
ROPE_BASE = 10000.0
NORM_EPS = 1e-6
NEG_INF = -1e30
LRU_C = 8.0
CONV_W = 4
CONV_PAD_LO = 2
LOG2E = 1.4426950408889634
LN2 = 0.6931471805599453
SCORE_SCALE = HEAD_DIM ** -0.5 * LOG2E
SQRT2 = 1.4142135623730951
F32_TINY = 1.1754943508222875e-38

SUBLANES = 8
LANES = 128
BF16_ROWS = 16
MOD_ROWS = 8
VMEM_LIMIT = 56 * 1024 * 1024
SCAN_STEPS = 64
NORM_ROW_BLOCKS = 4
ADALN_SEGS = 2
ROW_TILE_GROUP = 2
SCORE_LOOKAHEAD = 2


def _cparams(*sem):
    return pltpu.CompilerParams(dimension_semantics=sem, vmem_limit_bytes=VMEM_LIMIT)


def _tile(n, pref, unit):
    if n <= pref:
        return n
    t = (pref // unit) * unit
    while n % t:
        t -= unit
    return t


def _sigmoid(x):
    return 1.0 / (1.0 + jnp.exp(-x))


def _silu(x):
    return x * _sigmoid(x)


def _mod_kernel(c_ref, w_ref, b_ref, o_ref):
    s = _silu(c_ref[...]).astype(BF16)
    o_ref[...] = jnp.dot(s, w_ref[...].astype(BF16), preferred_element_type=F32) + b_ref[...]


def _mod_call(c8, w_mod, b_mod):
    depth, d, n3 = w_mod.shape
    tn = _tile(n3, 768, LANES)
    return pl.pallas_call(
        _mod_kernel,
        grid=(depth, n3 // tn),
        in_specs=[pl.BlockSpec((MOD_ROWS, d), lambda l, j: (0, 0)),
                  pl.BlockSpec((None, d, tn), lambda l, j: (l, 0, j)),
                  pl.BlockSpec((None, 1, tn), lambda l, j: (l, 0, j))],
        out_specs=pl.BlockSpec((None, MOD_ROWS, tn), lambda l, j: (l, 0, j)),
        out_shape=jax.ShapeDtypeStruct((depth, MOD_ROWS, n3), F32),
        compiler_params=_cparams("arbitrary", "arbitrary"),
        name="mod",
    )(c8, w_mod, b_mod.reshape(depth, 1, n3))


def _adaln_kernel(x_ref, sh_ref, sc_ref, nw_ref, o_ref, *, row):
    r = pl.program_id(0) if row is None else row
    x = x_ref[...]
    inv = lax.rsqrt(jnp.mean(x * x, axis=-1, keepdims=True) + NORM_EPS)
    xn = (x * inv) * nw_ref[...]
    o_ref[...] = (xn * (1.0 + sc_ref[pl.ds(r, 1), :]) + sh_ref[pl.ds(r, 1), :]).astype(o_ref.dtype)


def _adaln_call(x, shift, scale, nw, row):
    b, l, d = x.shape
    tr = _tile(l, 512, BF16_ROWS)
    return pl.pallas_call(
        functools.partial(_adaln_kernel, row=row),
        grid=(b, l // tr),
        in_specs=[pl.BlockSpec((None, tr, d), lambda i, j: (i, j, 0)),
                  pl.BlockSpec((MOD_ROWS, d), lambda i, j: (0, 0)),
                  pl.BlockSpec((MOD_ROWS, d), lambda i, j: (0, 0)),
                  pl.BlockSpec((1, d), lambda i, j: (0, 0))],
        out_specs=pl.BlockSpec((None, tr, d), lambda i, j: (i, j, 0)),
        out_shape=jax.ShapeDtypeStruct((b, l, d), BF16),
        compiler_params=_cparams("arbitrary", "arbitrary"),
        name="adaln",
    )(x, shift, scale, nw.reshape(1, d))


def _mm_kernel(a_ref, w_ref, o_ref):
    o_ref[...] = jnp.dot(a_ref[...], w_ref[...], preferred_element_type=F32).astype(o_ref.dtype)


def _mm_call(a, w, col0, ncols, out_dtype):
    m, k = a.shape
    tm = _tile(m, 1024, BF16_ROWS)
    tn = _tile(ncols, 512, LANES)
    assert col0 % tn == 0
    off = col0 // tn
    return pl.pallas_call(
        _mm_kernel,
        grid=(m // tm, ncols // tn),
        in_specs=[pl.BlockSpec((tm, k), lambda i, j: (i, 0)),
                  pl.BlockSpec((k, tn), lambda i, j: (0, j + off))],
        out_specs=pl.BlockSpec((tm, tn), lambda i, j: (i, j)),
        out_shape=jax.ShapeDtypeStruct((m, ncols), out_dtype),
        compiler_params=_cparams("arbitrary", "arbitrary"),
        name="matmul",
    )(a, w)


def _adaln_rows(x, sh_ref, sc_ref, nwk_ref, r):
    inv = lax.rsqrt(jnp.mean(x * x, axis=-1, keepdims=True) + NORM_EPS)
    xn = (x * inv) * nwk_ref[...]
    return (xn * (1.0 + sc_ref[pl.ds(r, 1), :]) + sh_ref[pl.ds(r, 1), :]).astype(BF16)


def _mm_qkvg_kernel(x_ref, sh_ref, sc_ref, nwk_ref, w_ref, cos_ref, sin_ref, nw_ref, o_ref, h_ref, *,
                    n_norm_tiles, tiles_per_seq):
    j = pl.program_id(1)
    slot = pl.program_id(2)
    tile = pl.program_id(0) * h_ref.shape[0] + slot
    tm = x_ref.shape[0]
    rb = tm // NORM_ROW_BLOCKS if tm % (NORM_ROW_BLOCKS * BF16_ROWS) == 0 else tm
    quarter = HEAD_DIM // 4

    def qk_rows(r0, a):
        y = jnp.dot(a, w_ref[...], preferred_element_type=F32)
        cos = cos_ref[r0:r0 + rb, :]
        sin = sin_ref[r0:r0 + rb, :]
        lane = lax.broadcasted_iota(jnp.int32, (rb, HEAD_DIM), 1)
        x1_lanes = (lane % (2 * quarter)) < quarter
        for h in range(y.shape[1] // HEAD_DIM):
            sl = slice(h * HEAD_DIM, (h + 1) * HEAD_DIM)
            x = y[:, sl]
            inv = lax.rsqrt(jnp.mean(x * x, axis=-1, keepdims=True) + NORM_EPS)
            xn = (x * inv) * nw_ref[:, sl]
            partner = jnp.where(x1_lanes, pltpu.roll(xn, HEAD_DIM - quarter, 1), pltpu.roll(xn, quarter, 1))
            o_ref[r0:r0 + rb, sl] = (xn * cos + partner * sin).astype(o_ref.dtype)

    @pl.when(j == 0)
    def _():
        r = tile // tiles_per_seq
        for r0 in range(0, tm, rb):
            a = _adaln_rows(x_ref[r0:r0 + rb, :], sh_ref, sc_ref, nwk_ref, r)
            h_ref[slot, r0:r0 + rb, :] = a
            qk_rows(r0, a)

    @pl.when((j > 0) & (j < n_norm_tiles))
    def _():
        for r0 in range(0, tm, rb):
            qk_rows(r0, h_ref[slot, r0:r0 + rb, :])

    @pl.when(j >= n_norm_tiles)
    def _():
        o_ref[...] = jnp.dot(h_ref[slot], w_ref[...], preferred_element_type=F32).astype(o_ref.dtype)


def _mm_qkvg_call(x, shift, scale, nwk, w, cos, sin, nw):
    b, s_len, k = x.shape
    m = b * s_len
    n = w.shape[1]
    tm = _tile(s_len, 512, BF16_ROWS)
    tn = _tile(nw.shape[1], 1024, HEAD_DIM)
    assert n % tn == 0 and nw.shape[1] >= tn
    n_norm_tiles = nw.shape[1] // tn
    t_per_seq = s_len // tm
    grp = ROW_TILE_GROUP if (m // tm) % ROW_TILE_GROUP == 0 else 1

    def x_map(i, j, s):
        return (i * grp + jnp.where(j == 0, s, grp - 1), 0)

    def pos_map(i, j, s):
        return ((i * grp + s) % t_per_seq, 0)

    return pl.pallas_call(
        functools.partial(_mm_qkvg_kernel, n_norm_tiles=n_norm_tiles, tiles_per_seq=t_per_seq),
        grid=(m // (tm * grp), n // tn, grp),
        in_specs=[pl.BlockSpec((tm, k), x_map),
                  pl.BlockSpec((MOD_ROWS, k), lambda i, j, s: (0, 0)),
                  pl.BlockSpec((MOD_ROWS, k), lambda i, j, s: (0, 0)),
                  pl.BlockSpec((1, k), lambda i, j, s: (0, 0)),
                  pl.BlockSpec((k, tn), lambda i, j, s: (0, j)),
                  pl.BlockSpec((tm, HEAD_DIM), pos_map),
                  pl.BlockSpec((tm, HEAD_DIM), pos_map),
                  pl.BlockSpec((1, tn), lambda i, j, s: (0, jnp.minimum(j, n_norm_tiles - 1)))],
        out_specs=pl.BlockSpec((tm, tn), lambda i, j, s: (i * grp + s, j)),
        out_shape=jax.ShapeDtypeStruct((m, n), BF16),
        scratch_shapes=[pltpu.VMEM((grp, tm, k), BF16)],
        compiler_params=_cparams("arbitrary", "arbitrary", "arbitrary"),
        name="matmul_qkvg",
    )(x.reshape(m, k), shift, scale, nwk.reshape(1, k), w, cos, sin, nw)


def _mm_ug_kernel(x_ref, sh_ref, sc_ref, nwk_ref, wu_ref, wg_ref, o_ref, h_ref, *, tmm, row, n_m):
    nbat, _, _, k = x_ref.shape
    j = pl.program_id(1)
    slot = pl.program_id(2)
    tile = pl.program_id(0) * h_ref.shape[0] + slot

    def project(bi, seg0, nseg, a):
        u = jnp.dot(a, wu_ref[...], preferred_element_type=F32).astype(BF16).astype(F32)
        g = jnp.dot(a, wg_ref[...], preferred_element_type=F32).astype(BF16).astype(F32)
        word = lax.bitcast_convert_type(u, jnp.uint32) | (lax.bitcast_convert_type(g, jnp.uint32) >> 16)
        for ds in range(nseg):
            for s in range(o_ref.shape[1]):
                o_ref[bi, s, pl.ds(seg0 + ds, tmm, stride=SUBLANES), :] = (
                    word[ds * tmm:(ds + 1) * tmm, s * LANES:(s + 1) * LANES])

    @pl.when(j == 0)
    def _():
        for bi in range(nbat):
            r = ((tile // n_m) * nbat + bi) if row is None else row
            for seg0 in range(0, SUBLANES, ADALN_SEGS):
                x = x_ref[bi, seg0:seg0 + ADALN_SEGS].reshape(ADALN_SEGS * tmm, k)
                a = _adaln_rows(x, sh_ref, sc_ref, nwk_ref, r)
                h_ref[slot, bi, seg0:seg0 + ADALN_SEGS] = a.reshape(ADALN_SEGS, tmm, k)
                project(bi, seg0, ADALN_SEGS, a)

    @pl.when(j > 0)
    def _():
        for bi in range(nbat):
            project(bi, 0, SUBLANES, h_ref[slot, bi].reshape(SUBLANES * tmm, k))


def _side_specs(side, steps, step_of):
    specs, shapes = [], []
    for t in side:
        rows = BF16_ROWS
        while t.shape[0] % rows or t.shape[0] // rows > steps:
            rows += BF16_ROWS
        last = t.shape[0] // rows - 1
        specs.append(pl.BlockSpec((rows, t.shape[1]),
                                  lambda *ids, last=last: (jnp.minimum(step_of(*ids), last), 0)))
        shapes.append(jax.ShapeDtypeStruct(t.shape, BF16))
    return specs, shapes


def _mm_ug_call(x, shift, scale, nwk, row, w, r):
    b, l, k = x.shape
    lseg = l // SUBLANES
    tmm = _tile(lseg, 64, BF16_ROWS)
    nbat = _tile(b, max(1, 64 // tmm), 1)
    tn = _tile(r, 512, LANES)
    nslab = tn // LANES
    n_m = lseg // tmm
    n_tiles = (b // nbat) * n_m
    grp = ROW_TILE_GROUP if n_tiles % ROW_TILE_GROUP == 0 else 1
    grid = (n_tiles // grp, r // tn, grp)

    def x_map(i, j, s):
        t = i * grp + jnp.where(j == 0, s, grp - 1)
        return (t // n_m, 0, t % n_m, 0)

    def out_map(i, j, s):
        t = i * grp + s
        return (t // n_m, j, t % n_m, 0)

    return pl.pallas_call(
        functools.partial(_mm_ug_kernel, tmm=tmm, row=row, n_m=n_m),
        grid=grid,
        in_specs=[pl.BlockSpec((nbat, SUBLANES, tmm, k), x_map),
                  pl.BlockSpec((MOD_ROWS, k), lambda i, j, s: (0, 0)),
                  pl.BlockSpec((MOD_ROWS, k), lambda i, j, s: (0, 0)),
                  pl.BlockSpec((1, k), lambda i, j, s: (0, 0)),
                  pl.BlockSpec((k, tn), lambda i, j, s: (0, j)),
                  pl.BlockSpec((k, tn), lambda i, j, s: (0, j + r // tn))],
        out_specs=pl.BlockSpec((nbat, nslab, SUBLANES * tmm, LANES), out_map),
        out_shape=jax.ShapeDtypeStruct((b, r // LANES, l, LANES), jnp.uint32),
        scratch_shapes=[pltpu.VMEM((grp, nbat, SUBLANES, tmm, k), BF16)],
        compiler_params=_cparams("arbitrary", "arbitrary", "arbitrary"),
        name="matmul_ug",
    )(x.reshape(b, SUBLANES, lseg, k), shift, scale, nwk.reshape(1, k), w, w)


def _mm_resid_kernel(a_ref, w_ref, x_ref, g_ref, o_ref, *, row, tiles_per_batch):
    r = (pl.program_id(0) // tiles_per_batch) if row is None else row
    y = jnp.dot(a_ref[...], w_ref[...], preferred_element_type=F32)
    o_ref[...] = x_ref[...] + g_ref[pl.ds(r, 1), :] * y


def _mm_resid_call(a, w, x, gate, rows_per_batch, row):
    m, k = a.shape
    n = w.shape[1]
    tm = _tile(rows_per_batch, 1024, BF16_ROWS)
    tn = _tile(n, 512, LANES)
    return pl.pallas_call(
        functools.partial(_mm_resid_kernel, row=row, tiles_per_batch=rows_per_batch // tm),
        grid=(m // tm, n // tn),
        in_specs=[pl.BlockSpec((tm, k), lambda i, j: (i, 0)),
                  pl.BlockSpec((k, tn), lambda i, j: (0, j)),
                  pl.BlockSpec((tm, tn), lambda i, j: (i, j)),
                  pl.BlockSpec((MOD_ROWS, tn), lambda i, j: (0, j))],
        out_specs=pl.BlockSpec((tm, tn), lambda i, j: (i, j)),
        out_shape=jax.ShapeDtypeStruct((m, n), F32),
        compiler_params=_cparams("arbitrary", "arbitrary"),
        name="matmul_resid",
    )(a, w, x, gate)


def _rglru_kernel(pl_ref, pc_ref, cw_ref, cb_ref, wg_ref, bg_ref, lam_ref, *rest, s_len, c_len, bw, n_side):
    side_in, (zl_ref, zc_ref) = rest[:n_side], rest[n_side:n_side + 2]
    side_out = rest[n_side + 2:2 * n_side + 2]
    cu_ref, hl_ref, pp_ref, zs_ref = rest[2 * n_side + 2:]
    for src, dst in zip(side_in, side_out):
        dst[...] = src[...].astype(dst.dtype)
    nslab = bw // LANES
    cw = cw_ref[...]
    cb = cb_ref[...]
    nlam = -lam_ref[...]
    softplus = jnp.maximum(nlam, 0.0) + jnp.log1p(jnp.exp(-jnp.abs(nlam)))
    k2 = (-0.5 * LRU_C * LOG2E) * softplus
    sub = lax.broadcasted_iota(jnp.int32, (SUBLANES, bw), 0)

    def unpack(p_ref, r0, n, low):
        parts = []
        for s in range(nslab):
            w = p_ref[s, pl.ds(r0, n), :]
            w = (w << 16) if low else (w & jnp.uint32(0xFFFF0000))
            parts.append(lax.bitcast_convert_type(w, F32))
        return jnp.concatenate(parts, axis=1)

    def seg_down(v):
        return jnp.where(sub == 0, 0.0, pltpu.roll(v, 1, 0))

    def seg_up(v):
        return jnp.where(sub == SUBLANES - 1, 0.0, pltpu.roll(v, SUBLANES - 1, 0))

    def conv_chunk(p_ref, k, n, n_chunks):
        length = n * n_chunks
        r0 = pl.multiple_of(k * n, n)
        lo = CONV_PAD_LO * SUBLANES
        hi = (CONV_W - 1 - CONV_PAD_LO) * SUBLANES
        cur = unpack(p_ref, r0, n, False)
        p0 = pl.multiple_of(jnp.where(k == 0, length - lo, r0 - lo), SUBLANES)
        n0 = pl.multiple_of(jnp.where(k == n_chunks - 1, 0, r0 + n), SUBLANES)
        prev = unpack(p_ref, p0, lo, False)
        nxt = unpack(p_ref, n0, hi, False)
        prev_edge = jnp.concatenate([seg_down(prev[i * SUBLANES:(i + 1) * SUBLANES])
                                     for i in range(CONV_PAD_LO)], axis=0)
        nxt_edge = jnp.concatenate([seg_up(nxt[i * SUBLANES:(i + 1) * SUBLANES])
                                    for i in range(CONV_W - 1 - CONV_PAD_LO)], axis=0)
        prev = jnp.where(k == 0, prev_edge, prev)
        nxt = jnp.where(k == n_chunks - 1, nxt_edge, nxt)
        ext = jnp.concatenate([prev, cur, nxt], axis=0)
        acc = cb + cw[0:1] * ext[0:n]
        for t in range(1, CONV_W):
            acc = acc + cw[t:t + 1] * ext[t * SUBLANES:t * SUBLANES + n]
        return acc

    def coeffs(uc, d):
        lo, hi = 2 * d * bw, 2 * (d + 1) * bw
        zg = jnp.dot(uc.astype(BF16), wg_ref[:, lo:hi], preferred_element_type=F32) + bg_ref[:, lo:hi]
        tr = jnp.tanh(zg[:, :bw])
        ti = jnp.tanh(zg[:, bw:])
        kd = k2[d:d + 1]
        la2 = kd * tr + kd
        a = jnp.exp2(la2)
        th = jnp.tanh(la2 * (-LN2))
        root = th * lax.rsqrt(jnp.maximum(th * th + th, F32_TINY))
        return a, root * ((ti + 1.0) * uc)

    def run_sequence(p_ref, z_ref, length, h0f, h0b):
        lseg = length // SUBLANES
        steps = _tile(lseg, SCAN_STEPS, BF16_ROWS)
        n = steps * SUBLANES
        n_chunks = length // n

        def conv_body(k, carry):
            cu_ref[pl.ds(pl.multiple_of(k * n, n), n), :] = conv_chunk(p_ref, k, n, n_chunks)
            return carry

        lax.fori_loop(0, n_chunks, conv_body, 0)

        def chunk_coeffs(k):
            return (coeffs(cu_ref[pl.ds(k * n, n), :], 0)
                    + coeffs(cu_ref[pl.ds((n_chunks - 1 - k) * n, n), :], 1))

        def chunk_scan(k, co, carry):
            af, bf, ab, bb = co
            hf, pf, hb, pb = carry
            rf = k * n
            rb = (n_chunks - 1 - k) * n
            hs_f, ps_f, hs_b, ps_b = [None] * steps, [None] * steps, [None] * steps, [None] * steps
            for m in range(steps):
                sf = slice(m * SUBLANES, (m + 1) * SUBLANES)
                hf = af[sf] * hf + bf[sf]
                pf = af[sf] * pf
                hs_f[m], ps_f[m] = hf, pf
                mb = steps - 1 - m
                sb = slice(mb * SUBLANES, (mb + 1) * SUBLANES)
                hb = ab[sb] * hb + bb[sb]
                pb = ab[sb] * pb
                hs_b[mb], ps_b[mb] = hb, pb
            hl_ref[0, pl.ds(rf, n), :] = jnp.concatenate(hs_f, axis=0)
            pp_ref[0, pl.ds(rf, n), :] = jnp.concatenate(ps_f, axis=0)
            hl_ref[1, pl.ds(rb, n), :] = jnp.concatenate(hs_b, axis=0)
            pp_ref[1, pl.ds(rb, n), :] = jnp.concatenate(ps_b, axis=0)
            return hf, pf, hb, pb

        carry = (jnp.zeros((SUBLANES, bw), F32), jnp.ones((SUBLANES, bw), F32),
                 jnp.zeros((SUBLANES, bw), F32), jnp.ones((SUBLANES, bw), F32))
        nxt = chunk_coeffs(0)
        for k in range(n_chunks):
            cur = nxt
            if k + 1 < n_chunks:
                nxt = chunk_coeffs(k + 1)
            carry = chunk_scan(k, cur, carry)
        ef, qf, eb, qb = carry

        rows_f, rows_b = [None] * SUBLANES, [None] * SUBLANES
        c = h0f
        for s in range(SUBLANES):
            rows_f[s] = c
            c = qf[s:s + 1] * c + ef[s:s + 1]
        hf_end = c
        c = h0b
        for s in range(SUBLANES - 1, -1, -1):
            rows_b[s] = c
            c = qb[s:s + 1] * c + eb[s:s + 1]
        hb_end = c
        cf = jnp.concatenate(rows_f, axis=0)[None]
        cbk = jnp.concatenate(rows_b, axis=0)[None]

        def out_body(k, carry):
            r0 = pl.multiple_of(k * n, n)
            sl = pl.ds(r0, n)
            hf = hl_ref[0, sl, :].reshape(steps, SUBLANES, bw) + pp_ref[0, sl, :].reshape(steps, SUBLANES, bw) * cf
            hb = hl_ref[1, sl, :].reshape(steps, SUBLANES, bw) + pp_ref[1, sl, :].reshape(steps, SUBLANES, bw) * cbk
            g = unpack(p_ref, r0, n, True)
            gate = (g * (0.25 * SQRT2)) * (jnp.tanh(g * 0.5) + 1.0)
            z = (hf + hb).reshape(n, bw) * gate
            for s in range(nslab):
                zs_ref[s, 0:n, :] = z[:, s * LANES:(s + 1) * LANES]
            for seg in range(SUBLANES):
                piece = jnp.concatenate([zs_ref[s, pl.ds(seg, steps, stride=SUBLANES), :]
                                         for s in range(nslab)], axis=1)
                t0 = pl.multiple_of(seg * lseg + k * steps, steps)
                z_ref[pl.ds(t0, steps), :] = piece.astype(z_ref.dtype)
            return carry

        lax.fori_loop(0, n_chunks, out_body, 0)
        return hf_end, hb_end

    zero = jnp.zeros((1, bw), F32)
    hf0, hb0 = run_sequence(pc_ref, zc_ref, c_len, zero, zero)
    run_sequence(pl_ref, zl_ref, s_len, hf0, hb0)


def _rglru_call(p_lat, p_ctx, conv_w, conv_b, w_gate, b_gate, lam, side=()):
    b, nsl, s_len, _ = p_lat.shape
    c_len = p_ctx.shape[2]
    r = nsl * LANES
    nblk, bw, _ = w_gate.shape
    nslab = bw // LANES
    lmax = max(s_len, c_len)
    chunk = SUBLANES * SCAN_STEPS
    side_specs, side_shapes = _side_specs(side, b * nblk, lambda i, j: i * nblk + j)
    kern = functools.partial(_rglru_kernel, s_len=s_len, c_len=c_len, bw=bw, n_side=len(side))
    out = pl.pallas_call(
        kern,
        grid=(b, nblk),
        in_specs=[pl.BlockSpec((None, nslab, s_len, LANES), lambda i, j: (i, j, 0, 0)),
                  pl.BlockSpec((None, nslab, c_len, LANES), lambda i, j: (i, j, 0, 0)),
                  pl.BlockSpec((CONV_W, bw), lambda i, j: (0, j)),
                  pl.BlockSpec((1, bw), lambda i, j: (0, j)),
                  pl.BlockSpec((None, bw, 4 * bw), lambda i, j: (j, 0, 0)),
                  pl.BlockSpec((None, 1, 4 * bw), lambda i, j: (j, 0, 0)),
                  pl.BlockSpec((2, bw), lambda i, j: (0, j))] + side_specs,
        out_specs=[pl.BlockSpec((None, s_len, bw), lambda i, j: (i, 0, j)),
                   pl.BlockSpec((None, c_len, bw), lambda i, j: (i, 0, j))] + side_specs,
        out_shape=[jax.ShapeDtypeStruct((b, s_len, r), BF16),
                   jax.ShapeDtypeStruct((b, c_len, r), BF16)] + side_shapes,
        scratch_shapes=[pltpu.VMEM((lmax, bw), F32),
                        pltpu.VMEM((2, lmax, bw), F32),
                        pltpu.VMEM((2, lmax, bw), F32),
                        pltpu.VMEM((nslab, chunk, LANES), F32)],
        compiler_params=_cparams("arbitrary", "arbitrary"),
        name="rglru",
    )(p_lat, p_ctx, conv_w, conv_b.reshape(1, r), w_gate, b_gate, lam, *side)
    return out[0], out[1], out[2:]


def _headnorm_kernel(x_ref, w_ref, o_ref, *, n_heads):
    w = w_ref[...]
    for h in range(n_heads):
        sl = slice(h * HEAD_DIM, (h + 1) * HEAD_DIM)
        x = x_ref[:, sl].astype(F32)
        inv = lax.rsqrt(jnp.mean(x * x, axis=-1, keepdims=True) + NORM_EPS)
        o_ref[:, sl] = ((x * inv) * w).astype(o_ref.dtype)


def _qkprep_call(x, col0, width, w):
    b, l, _ = x.shape
    tr = _tile(l, 256, BF16_ROWS)
    assert col0 % width == 0
    off = col0 // width
    return pl.pallas_call(
        functools.partial(_headnorm_kernel, n_heads=width // HEAD_DIM),
        grid=(b, l // tr),
        in_specs=[pl.BlockSpec((None, tr, width), lambda i, j: (i, j, off)),
                  pl.BlockSpec((1, HEAD_DIM), lambda i, j: (0, 0))],
        out_specs=pl.BlockSpec((None, tr, width), lambda i, j: (i, j, 0)),
        out_shape=jax.ShapeDtypeStruct((b, l, width), BF16),
        compiler_params=_cparams("arbitrary", "arbitrary"),
        name="ctx_key_norm",
    )(x, w.reshape(1, HEAD_DIM))


def _dot_nt(a, b):
    return lax.dot_general(a, b, (((1,), (1,)), ((), ())), preferred_element_type=F32)


def _attn_kernel(sink_ref, bias_ref, q_ref, kp_ref, km_ref, kn_ref, vp_ref, vm_ref, vn_ref, kc_ref, vc_ref,
                 g_ref, o_ref, *, groups, qblocks, n_steps):
    kvh = pl.program_id(1)
    step = pl.program_id(2)
    blk = ATTN_BLOCK
    kw = jnp.concatenate([kp_ref[...], km_ref[...], kn_ref[...]], axis=0)
    vw = jnp.concatenate([vp_ref[...], vm_ref[...], vn_ref[...]], axis=0)
    kc = kc_ref[...]
    vt_c = jnp.concatenate([vc_ref[...].T, jnp.ones((BF16_ROWS, kc.shape[0]), BF16)], axis=0)
    vt_w = jnp.concatenate([vw.T, jnp.ones((BF16_ROWS, vw.shape[0]), BF16)], axis=0)
    sink2 =jnp.concatenate([jnp.full((1, blk), sink_ref[kvh * groups + g] * LOG2E, F32)
                             for g in range(groups)], axis=1)
    def scores(i):
        q = q_ref[i * blk:(i + 1) * blk, :]
        qs = jnp.concatenate([q[:, g * HEAD_DIM:(g + 1) * HEAD_DIM] for g in range(groups)], axis=0)
        first = (step == 0).astype(jnp.int32) if i == 0 else 0
        last = (step == n_steps - 1).astype(jnp.int32) if i == qblocks - 1 else 0
        bias = bias_ref[first + 2 * last]
        return _dot_nt(kc, qs), _dot_nt(kw[i * blk:(i + 3) * blk], qs) + bias

    ahead = [scores(i) for i in range(min(SCORE_LOOKAHEAD, qblocks))]
    for i in range(qblocks):
        s_c, s_w = ahead.pop(0)
        if i + SCORE_LOOKAHEAD < qblocks:
            ahead.append(scores(i + SCORE_LOOKAHEAD))
        m = jnp.maximum(jnp.maximum(jnp.max(s_c, axis=0, keepdims=True),
                                    jnp.max(s_w, axis=0, keepdims=True)), sink2)
        p_c = jnp.exp2(s_c - m).astype(BF16)
        p_w = jnp.exp2(s_w - m).astype(BF16)
        ot = (jnp.dot(vt_c, p_c, preferred_element_type=F32)
              + jnp.dot(vt_w[:, i * blk:(i + 3) * blk], p_w, preferred_element_type=F32))
        den = ot[HEAD_DIM:HEAD_DIM + 1, :] + jnp.exp2(sink2 - m)
        ot = ot[:HEAD_DIM, :] * (1.0 / den)
        o = jnp.concatenate([ot[:, g * blk:(g + 1) * blk].T for g in range(groups)], axis=1)
        gate = _silu(g_ref[i * blk:(i + 1) * blk, :].astype(F32))
        o_ref[i * blk:(i + 1) * blk, :] = (o * gate).astype(o_ref.dtype)


def _band_bias(groups):
    blk = ATTN_BLOCK
    ki = jnp.arange(3 * blk)[:, None]
    qi = jnp.arange(blk)[None, :]
    band = jnp.abs(ki - blk - qi) <= WINDOW
    out = []
    for var in range(4):
        ok = band & ((ki >= blk) | (var % 2 == 0)) & ((ki < 2 * blk) | (var // 2 == 0))
        out.append(jnp.tile(jnp.where(ok, 0.0, NEG_INF).astype(F32), (1, groups)))
    return jnp.stack(out)


def _attn_call(sink, kcn, qkvg, kvc, attn_w, kv_w):
    b, s_len, _ = qkvg.shape
    c_len = kcn.shape[1]
    n_kv = kv_w // HEAD_DIM
    groups = attn_w // kv_w
    gw = groups * HEAD_DIM
    blk = ATTN_BLOCK
    nb = s_len // blk
    qblocks = next(q for q in (16, 8, 4, 2, 1) if nb % q == 0)
    n_steps = nb // qblocks
    k_off = attn_w // HEAD_DIM
    v_off = (attn_w + kv_w) // HEAD_DIM
    g_off = (attn_w + 2 * kv_w) // gw
    assert (attn_w + 2 * kv_w) % gw == 0

    def edge_spec(prev, off):
        if prev:
            return pl.BlockSpec((None, blk, HEAD_DIM),
                                lambda i, h, j: (i, jnp.maximum(j * qblocks - 1, 0), h + off))
        return pl.BlockSpec((None, blk, HEAD_DIM),
                            lambda i, h, j: (i, jnp.minimum((j + 1) * qblocks, nb - 1), h + off))

    def main_spec(off):
        return pl.BlockSpec((None, qblocks * blk, HEAD_DIM), lambda i, h, j: (i, j, h + off))

    def tile_spec(off):
        return pl.BlockSpec((None, qblocks * blk, gw), lambda i, h, j: (i, j, h + off))

    return pl.pallas_call(
        functools.partial(_attn_kernel, groups=groups, qblocks=qblocks, n_steps=n_steps),
        grid=(b, n_kv, n_steps),
        in_specs=[pl.BlockSpec(memory_space=pltpu.SMEM),
                  pl.BlockSpec((4, 3 * blk, groups * blk), lambda i, h, j: (0, 0, 0)),
                  tile_spec(0),
                  edge_spec(True, k_off), main_spec(k_off), edge_spec(False, k_off),
                  edge_spec(True, v_off), main_spec(v_off), edge_spec(False, v_off),
                  pl.BlockSpec((None, c_len, HEAD_DIM), lambda i, h, j: (i, 0, h)),
                  pl.BlockSpec((None, c_len, HEAD_DIM), lambda i, h, j: (i, 0, h + n_kv)),
                  tile_spec(g_off)],
        out_specs=tile_spec(0),
        out_shape=jax.ShapeDtypeStruct((b, s_len, attn_w), BF16),
        compiler_params=_cparams("arbitrary", "arbitrary", "arbitrary"),
        name="attention",
    )(sink, _band_bias(groups), qkvg, qkvg, qkvg, qkvg, qkvg, qkvg, qkvg, kcn, kvc, qkvg)


def _rope_tables(s_len):
    t = jnp.arange(s_len, dtype=jnp.int32)
    pos = jnp.stack([t // GRID_W, t % GRID_W], axis=1).astype(F32)
    half = HEAD_DIM // 2
    quarter = half // 2
    inv_freq = ROPE_BASE ** (-jnp.arange(quarter, dtype=F32) * (2.0 / half))
    ang = pos[:, :, None] * inv_freq[None, None, :]
    cos = jnp.concatenate([jnp.cos(ang), jnp.cos(ang)], axis=-1).reshape(s_len, HEAD_DIM)
    sin = jnp.concatenate([-jnp.sin(ang), jnp.sin(ang)], axis=-1).reshape(s_len, HEAD_DIM)
    return cos, sin


def kernel(x, c, ctx, c_ctx, w_mod, b_mod, norm_w, rg_w_in, rg_conv_w, rg_conv_b, rg_w_r, rg_b_r, rg_w_i,
           rg_b_i, rg_lam, rg_w_out, at_w_in, at_q_norm, at_k_norm, at_sink, at_w_out):
    b, s_len, d = x.shape
    c_len = ctx.shape[1]
    depth = w_mod.shape[0]
    assert depth == 2 and b < MOD_ROWS
    ctx_row = b

    c8 = jnp.concatenate([c, c_ctx[None, :], jnp.zeros((MOD_ROWS - b - 1, d), F32)], axis=0)
    mod = _mod_call(c8, w_mod, b_mod)
    shift, scale, gate = mod[:, :, :d], mod[:, :, d:2 * d], mod[:, :, 2 * d:]

    r = rg_w_out.shape[1]
    w_in = rg_w_in[0].astype(BF16)
    p_lat = _mm_ug_call(x, shift[0], scale[0], norm_w[0], None, w_in, r)
    p_ctx = _mm_ug_call(ctx, shift[0], scale[0], norm_w[0], ctx_row, w_in, r)
    w_gate = (0.5 * jnp.concatenate([rg_w_r[0, 0], rg_w_i[0, 0], rg_w_r[0, 1], rg_w_i[0, 1]], axis=-1)).astype(BF16)
    nblk, bw = rg_w_r.shape[2], rg_w_r.shape[3]
    b_gate = 0.5 * jnp.concatenate([rg_b_r[0, 0].reshape(nblk, 1, bw), rg_b_i[0, 0].reshape(nblk, 1, bw),
                                    rg_b_r[0, 1].reshape(nblk, 1, bw), rg_b_i[0, 1].reshape(nblk, 1, bw)], axis=-1)
    z, zc, (w_out, at_w_in_bf, at_w_out_bf) = _rglru_call(
        p_lat, p_ctx, rg_conv_w[0], rg_conv_b[0], w_gate, b_gate, rg_lam[0],
        (rg_w_out[0], at_w_in[0], at_w_out[0]))
    x = _mm_resid_call(z.reshape(b * s_len, r), w_out, x.reshape(b * s_len, d), gate[0], s_len, None)
    x = x.reshape(b, s_len, d)
    ctx = _mm_resid_call(zc.reshape(b * c_len, r), w_out, ctx.reshape(b * c_len, d), gate[0], c_len, ctx_row)
    ctx = ctx.reshape(b, c_len, d)

    attn_w = at_w_out.shape[1]
    kv_w = (at_w_in.shape[2] - 2 * attn_w) // 2
    hc = _adaln_call(ctx, shift[1], scale[1], norm_w[1], ctx_row)
    n_heads, n_kv = attn_w // HEAD_DIM, kv_w // HEAD_DIM
    w_in = at_w_in_bf
    k_norm = at_k_norm[0]
    nw = jnp.concatenate([jnp.tile(at_q_norm[0] * SCORE_SCALE, n_heads), jnp.tile(k_norm, n_kv)])
    nw = nw.reshape(1, attn_w + kv_w)
    cos, sin = _rope_tables(s_len)
    qkvg = _mm_qkvg_call(x, shift[1], scale[1], norm_w[1], w_in, cos, sin, nw).reshape(b, s_len, -1)
    kvc = _mm_call(hc.reshape(b * c_len, d), w_in, attn_w, 2 * kv_w, BF16).reshape(b, c_len, 2 * kv_w)
    kcn = _qkprep_call(kvc, 0, kv_w, k_norm)
    z = _attn_call(at_sink[0], kcn, qkvg, kvc, attn_w, kv_w)
    x = _mm_resid_call(z.reshape(b * s_len, attn_w), at_w_out_bf, x.reshape(b * s_len, d),
                       gate[1], s_len, None)
    return x.reshape(b, s_len, d)
```

```python
import functools

import jax
import jax.numpy as jnp
from jax import lax
from jax.experimental import pallas as pl
from jax.experimental.pallas import tpu as pltpu

F32 = jnp.float32
BF16 = jnp.bfloat16

HEAD_DIM = 128
WINDOW = 128
ATTN_BLOCK = 128
GRID_W = 64
ROPE_BASE = 10000.0
NORM_EPS = 1e-6
NEG_INF = -1e30
LRU_C = 8.0
CONV_W = 4
CONV_PAD_LO = 2
LOG2E = 1.4426950408889634
LN2 = 0.6931471805599453
SCORE_SCALE = HEAD_DIM ** -0.5 * LOG2E
SQRT2 = 1.4142135623730951
F32_TINY = 1.1754943508222875e-38

SUBLANES = 8
LANES = 128
BF16_ROWS = 16
MOD_ROWS = 8
VMEM_LIMIT = 56 * 1024 * 1024
SCAN_STEPS = 64
NORM_ROW_BLOCKS = 4
ADALN_SEGS = 2
ROW_TILE_GROUP = 2
SCORE_LOOKAHEAD = 3
ATTN_HEADS_PER_UNIT = 2


def _cparams(*sem):
    return pltpu.CompilerParams(dimension_semantics=sem, vmem_limit_bytes=VMEM_LIMIT)


def _tile(n, pref, unit):
    if n <= pref:
        return n
    t = (pref // unit) * unit
    while n % t:
        t -= unit
    return t


def _sigmoid(x):
    return 1.0 / (1.0 + jnp.exp(-x))


def _silu(x):
    return x * _sigmoid(x)


def _mod_kernel(c_ref, w_ref, b_ref, o_ref):
    s = _silu(c_ref[...]).astype(BF16)
    o_ref[...] = jnp.dot(s, w_ref[...].astype(BF16), preferred_element_type=F32) + b_ref[...]


def _mod_call(c8, w_mod, b_mod):
    depth, d, n3 = w_mod.shape
    tn = _tile(n3, 768, LANES)
    return pl.pallas_call(
        _mod_kernel,
        grid=(depth, n3 // tn),
        in_specs=[pl.BlockSpec((MOD_ROWS, d), lambda l, j: (0, 0)),
                  pl.BlockSpec((None, d, tn), lambda l, j: (l, 0, j)),
                  pl.BlockSpec((None, 1, tn), lambda l, j: (l, 0, j))],
        out_specs=pl.BlockSpec((None, MOD_ROWS, tn), lambda l, j: (l, 0, j)),
        out_shape=jax.ShapeDtypeStruct((depth, MOD_ROWS, n3), F32),
        compiler_params=_cparams("arbitrary", "arbitrary"),
        name="mod",
    )(c8, w_mod, b_mod.reshape(depth, 1, n3))


def _adaln_kernel(x_ref, sh_ref, sc_ref, nw_ref, o_ref, *, row):
    r = pl.program_id(0) if row is None else row
    x = x_ref[...]
    inv = lax.rsqrt(jnp.mean(x * x, axis=-1, keepdims=True) + NORM_EPS)
    xn = (x * inv) * nw_ref[...]
    o_ref[...] = (xn * (1.0 + sc_ref[pl.ds(r, 1), :]) + sh_ref[pl.ds(r, 1), :]).astype(o_ref.dtype)


def _adaln_call(x, shift, scale, nw, row):
    b, l, d = x.shape
    tr = _tile(l, 512, BF16_ROWS)
    return pl.pallas_call(
        functools.partial(_adaln_kernel, row=row),
        grid=(b, l // tr),
        in_specs=[pl.BlockSpec((None, tr, d), lambda i, j: (i, j, 0)),
                  pl.BlockSpec((MOD_ROWS, d), lambda i, j: (0, 0)),
                  pl.BlockSpec((MOD_ROWS, d), lambda i, j: (0, 0)),
                  pl.BlockSpec((1, d), lambda i, j: (0, 0))],
        out_specs=pl.BlockSpec((None, tr, d), lambda i, j: (i, j, 0)),
        out_shape=jax.ShapeDtypeStruct((b, l, d), BF16),
        compiler_params=_cparams("arbitrary", "arbitrary"),
        name="adaln",
    )(x, shift, scale, nw.reshape(1, d))


def _mm_kernel(a_ref, w_ref, o_ref):
    o_ref[...] = jnp.dot(a_ref[...], w_ref[...], preferred_element_type=F32).astype(o_ref.dtype)


def _mm_call(a, w, col0, ncols, out_dtype):
    m, k = a.shape
    tm = _tile(m, 1024, BF16_ROWS)
    tn = _tile(ncols, 512, LANES)
    assert col0 % tn == 0
    off = col0 // tn
    return pl.pallas_call(
        _mm_kernel,
        grid=(m // tm, ncols // tn),
        in_specs=[pl.BlockSpec((tm, k), lambda i, j: (i, 0)),
                  pl.BlockSpec((k, tn), lambda i, j: (0, j + off))],
        out_specs=pl.BlockSpec((tm, tn), lambda i, j: (i, j)),
        out_shape=jax.ShapeDtypeStruct((m, ncols), out_dtype),
        compiler_params=_cparams("arbitrary", "arbitrary"),
        name="matmul",
    )(a, w)


def _adaln_rows(x, sh_ref, sc_ref, nwk_ref, r):
    inv = lax.rsqrt(jnp.mean(x * x, axis=-1, keepdims=True) + NORM_EPS)
    xn = (x * inv) * nwk_ref[...]
    return (xn * (1.0 + sc_ref[pl.ds(r, 1), :]) + sh_ref[pl.ds(r, 1), :]).astype(BF16)


def _mm_qkvg_kernel(x_ref, sh_ref, sc_ref, nwk_ref, w_ref, cos_ref, sin_ref, nw_ref, o_ref, h_ref, *,
                    n_norm_tiles, tiles_per_seq):
    j = pl.program_id(1)
    slot = pl.program_id(2)
    tile = pl.program_id(0) * h_ref.shape[0] + slot
    tm = x_ref.shape[0]
    rb = tm // NORM_ROW_BLOCKS if tm % (NORM_ROW_BLOCKS * BF16_ROWS) == 0 else tm
    quarter = HEAD_DIM // 4

    def qk_rows(r0, a):
        y = jnp.dot(a, w_ref[...], preferred_element_type=F32)
        cos = cos_ref[r0:r0 + rb, :]
        sin = sin_ref[r0:r0 + rb, :]
        lane = lax.broadcasted_iota(jnp.int32, (rb, HEAD_DIM), 1)
        x1_lanes = (lane % (2 * quarter)) < quarter
        for h in range(y.shape[1] // HEAD_DIM):
            sl = slice(h * HEAD_DIM, (h + 1) * HEAD_DIM)
            x = y[:, sl]
            inv = lax.rsqrt(jnp.mean(x * x, axis=-1, keepdims=True) + NORM_EPS)
            xn = (x * inv) * nw_ref[:, sl]
            partner = jnp.where(x1_lanes, pltpu.roll(xn, HEAD_DIM - quarter, 1), pltpu.roll(xn, quarter, 1))
            o_ref[r0:r0 + rb, sl] = (xn * cos + partner * sin).astype(o_ref.dtype)

    @pl.when(j == 0)
    def _():
        r = tile // tiles_per_seq
        for r0 in range(0, tm, rb):
            a = _adaln_rows(x_ref[r0:r0 + rb, :], sh_ref, sc_ref, nwk_ref, r)
            h_ref[slot, r0:r0 + rb, :] = a
            qk_rows(r0, a)

    @pl.when((j > 0) & (j < n_norm_tiles))
    def _():
        for r0 in range(0, tm, rb):
            qk_rows(r0, h_ref[slot, r0:r0 + rb, :])

    @pl.when(j >= n_norm_tiles)
    def _():
        o_ref[...] = jnp.dot(h_ref[slot], w_ref[...], preferred_element_type=F32).astype(o_ref.dtype)


def _mm_qkvg_call(x, shift, scale, nwk, w, cos, sin, nw):
    b, s_len, k = x.shape
    m = b * s_len
    n = w.shape[1]
    tm = _tile(s_len, 512, BF16_ROWS)
    tn = _tile(nw.shape[1], 1024, HEAD_DIM)
    assert n % tn == 0 and nw.shape[1] >= tn
    n_norm_tiles = nw.shape[1] // tn
    t_per_seq = s_len // tm
    grp = ROW_TILE_GROUP if (m // tm) % ROW_TILE_GROUP == 0 else 1

    def x_map(i, j, s):
        return (i * grp + jnp.where(j == 0, s, grp - 1), 0)

    def pos_map(i, j, s):
        return ((i * grp + s) % t_per_seq, 0)

    return pl.pallas_call(
        functools.partial(_mm_qkvg_kernel, n_norm_tiles=n_norm_tiles, tiles_per_seq=t_per_seq),
        grid=(m // (tm * grp), n // tn, grp),
        in_specs=[pl.BlockSpec((tm, k), x_map),
                  pl.BlockSpec((MOD_ROWS, k), lambda i, j, s: (0, 0)),
                  pl.BlockSpec((MOD_ROWS, k), lambda i, j, s: (0, 0)),
                  pl.BlockSpec((1, k), lambda i, j, s: (0, 0)),
                  pl.BlockSpec((k, tn), lambda i, j, s: (0, j)),
                  pl.BlockSpec((tm, HEAD_DIM), pos_map),
                  pl.BlockSpec((tm, HEAD_DIM), pos_map),
                  pl.BlockSpec((1, tn), lambda i, j, s: (0, jnp.minimum(j, n_norm_tiles - 1)))],
        out_specs=pl.BlockSpec((tm, tn), lambda i, j, s: (i * grp + s, j)),
        out_shape=jax.ShapeDtypeStruct((m, n), BF16),
        scratch_shapes=[pltpu.VMEM((grp, tm, k), BF16)],
        compiler_params=_cparams("arbitrary", "arbitrary", "arbitrary"),
        name="matmul_qkvg",
    )(x.reshape(m, k), shift, scale, nwk.reshape(1, k), w, cos, sin, nw)


def _mm_ug_kernel(x_ref, sh_ref, sc_ref, nwk_ref, wu_ref, wg_ref, o_ref, h_ref, *, tmm, row, n_m):
    nbat, _, _, k = x_ref.shape
    j = pl.program_id(1)
    slot = pl.program_id(2)
    tile = pl.program_id(0) * h_ref.shape[0] + slot

    def project(bi, seg0, nseg, a):
        u = jnp.dot(a, wu_ref[...], preferred_element_type=F32).astype(BF16).astype(F32)
        g = jnp.dot(a, wg_ref[...], preferred_element_type=F32).astype(BF16).astype(F32)
        word = lax.bitcast_convert_type(u, jnp.uint32) | (lax.bitcast_convert_type(g, jnp.uint32) >> 16)
        for ds in range(nseg):
            for s in range(o_ref.shape[1]):
                o_ref[bi, s, pl.ds(seg0 + ds, tmm, stride=SUBLANES), :] = (
                    word[ds * tmm:(ds + 1) * tmm, s * LANES:(s + 1) * LANES])

    @pl.when(j == 0)
    def _():
        for bi in range(nbat):
            r = ((tile // n_m) * nbat + bi) if row is None else row
            for seg0 in range(0, SUBLANES, ADALN_SEGS):
                x = x_ref[bi, seg0:seg0 + ADALN_SEGS].reshape(ADALN_SEGS * tmm, k)
                a = _adaln_rows(x, sh_ref, sc_ref, nwk_ref, r)
                h_ref[slot, bi, seg0:seg0 + ADALN_SEGS] = a.reshape(ADALN_SEGS, tmm, k)
                project(bi, seg0, ADALN_SEGS, a)

    @pl.when(j > 0)
    def _():
        for bi in range(nbat):
            project(bi, 0, SUBLANES, h_ref[slot, bi].reshape(SUBLANES * tmm, k))


def _side_specs(side, steps, step_of):
    specs, shapes = [], []
    for t in side:
        rows = BF16_ROWS
        while t.shape[0] % rows or t.shape[0] // rows > steps:
            rows += BF16_ROWS
        last = t.shape[0] // rows - 1
        specs.append(pl.BlockSpec((rows, t.shape[1]),
                                  lambda *ids, last=last: (jnp.minimum(step_of(*ids), last), 0)))
        shapes.append(jax.ShapeDtypeStruct(t.shape, BF16))
    return specs, shapes


def _mm_ug_call(x, shift, scale, nwk, row, w, r):
    b, l, k = x.shape
    lseg = l // SUBLANES
    tmm = _tile(lseg, 64, BF16_ROWS)
    nbat = _tile(b, max(1, 64 // tmm), 1)
    tn = _tile(r, 512, LANES)
    nslab = tn // LANES
    n_m = lseg // tmm
    n_tiles = (b // nbat) * n_m
    grp = ROW_TILE_GROUP if n_tiles % ROW_TILE_GROUP == 0 else 1
    grid = (n_tiles // grp, r // tn, grp)

    def x_map(i, j, s):
        t = i * grp + jnp.where(j == 0, s, grp - 1)
        return (t // n_m, 0, t % n_m, 0)

    def out_map(i, j, s):
        t = i * grp + s
        return (t // n_m, j, t % n_m, 0)

    return pl.pallas_call(
        functools.partial(_mm_ug_kernel, tmm=tmm, row=row, n_m=n_m),
        grid=grid,
        in_specs=[pl.BlockSpec((nbat, SUBLANES, tmm, k), x_map),
                  pl.BlockSpec((MOD_ROWS, k), lambda i, j, s: (0, 0)),
                  pl.BlockSpec((MOD_ROWS, k), lambda i, j, s: (0, 0)),
                  pl.BlockSpec((1, k), lambda i, j, s: (0, 0)),
                  pl.BlockSpec((k, tn), lambda i, j, s: (0, j)),
                  pl.BlockSpec((k, tn), lambda i, j, s: (0, j + r // tn))],
        out_specs=pl.BlockSpec((nbat, nslab, SUBLANES * tmm, LANES), out_map),
        out_shape=jax.ShapeDtypeStruct((b, r // LANES, l, LANES), jnp.uint32),
        scratch_shapes=[pltpu.VMEM((grp, nbat, SUBLANES, tmm, k), BF16)],
        compiler_params=_cparams("arbitrary", "arbitrary", "arbitrary"),
        name="matmul_ug",
    )(x.reshape(b, SUBLANES, lseg, k), shift, scale, nwk.reshape(1, k), w, w)


def _mm_resid_kernel(a_ref, w_ref, x_ref, g_ref, o_ref, *, row, tiles_per_batch):
    r = (pl.program_id(0) // tiles_per_batch) if row is None else row
    y = jnp.dot(a_ref[...], w_ref[...], preferred_element_type=F32)
    o_ref[...] = x_ref[...] + g_ref[pl.ds(r, 1), :] * y


def _mm_resid_call(a, w, x, gate, rows_per_batch, row):
    m, k = a.shape
    n = w.shape[1]
    tm = _tile(rows_per_batch, 1024, BF16_ROWS)
    tn = _tile(n, 512, LANES)
    return pl.pallas_call(
        functools.partial(_mm_resid_kernel, row=row, tiles_per_batch=rows_per_batch // tm),
        grid=(m // tm, n // tn),
        in_specs=[pl.BlockSpec((tm, k), lambda i, j: (i, 0)),
                  pl.BlockSpec((k, tn), lambda i, j: (0, j)),
                  pl.BlockSpec((tm, tn), lambda i, j: (i, j)),
                  pl.BlockSpec((MOD_ROWS, tn), lambda i, j: (0, j))],
        out_specs=pl.BlockSpec((tm, tn), lambda i, j: (i, j)),
        out_shape=jax.ShapeDtypeStruct((m, n), F32),
        compiler_params=_cparams("arbitrary", "arbitrary"),
        name="matmul_resid",
    )(a, w, x, gate)


def _rglru_kernel(pl_ref, pc_ref, cw_ref, cb_ref, wg_ref, bg_ref, lam_ref, *rest, s_len, c_len, bw, n_side):
    side_in, (zl_ref, zc_ref) = rest[:n_side], rest[n_side:n_side + 2]
    side_out = rest[n_side + 2:2 * n_side + 2]
    cu_ref, hl_ref, pp_ref, zs_ref = rest[2 * n_side + 2:]
    for src, dst in zip(side_in, side_out):
        dst[...] = src[...].astype(dst.dtype)
    nslab = bw // LANES
    cw = cw_ref[...]
    cb = cb_ref[...]
    nlam = -lam_ref[...]
    softplus = jnp.maximum(nlam, 0.0) + jnp.log1p(jnp.exp(-jnp.abs(nlam)))
    k2 = (-0.5 * LRU_C * LOG2E) * softplus
    sub = lax.broadcasted_iota(jnp.int32, (SUBLANES, bw), 0)

    def unpack(p_ref, r0, n, low):
        parts = []
        for s in range(nslab):
            w = p_ref[s, pl.ds(r0, n), :]
            w = (w << 16) if low else (w & jnp.uint32(0xFFFF0000))
            parts.append(lax.bitcast_convert_type(w, F32))
        return jnp.concatenate(parts, axis=1)

    def seg_down(v):
        return jnp.where(sub == 0, 0.0, pltpu.roll(v, 1, 0))

    def seg_up(v):
        return jnp.where(sub == SUBLANES - 1, 0.0, pltpu.roll(v, SUBLANES - 1, 0))

    def conv_chunk(p_ref, k, n, n_chunks):
        length = n * n_chunks
        r0 = pl.multiple_of(k * n, n)
        lo = CONV_PAD_LO * SUBLANES
        hi = (CONV_W - 1 - CONV_PAD_LO) * SUBLANES
        cur = unpack(p_ref, r0, n, False)
        p0 = pl.multiple_of(jnp.where(k == 0, length - lo, r0 - lo), SUBLANES)
        n0 = pl.multiple_of(jnp.where(k == n_chunks - 1, 0, r0 + n), SUBLANES)
        prev = unpack(p_ref, p0, lo, False)
        nxt = unpack(p_ref, n0, hi, False)
        prev_edge = jnp.concatenate([seg_down(prev[i * SUBLANES:(i + 1) * SUBLANES])
                                     for i in range(CONV_PAD_LO)], axis=0)
        nxt_edge = jnp.concatenate([seg_up(nxt[i * SUBLANES:(i + 1) * SUBLANES])
                                    for i in range(CONV_W - 1 - CONV_PAD_LO)], axis=0)
        prev = jnp.where(k == 0, prev_edge, prev)
        nxt = jnp.where(k == n_chunks - 1, nxt_edge, nxt)
        ext = jnp.concatenate([prev, cur, nxt], axis=0)
        acc = cb + cw[0:1] * ext[0:n]
        for t in range(1, CONV_W):
            acc = acc + cw[t:t + 1] * ext[t * SUBLANES:t * SUBLANES + n]
        return acc

    def coeffs(uc, d):
        lo, hi = 2 * d * bw, 2 * (d + 1) * bw
        zg = jnp.dot(uc.astype(BF16), wg_ref[:, lo:hi], preferred_element_type=F32) + bg_ref[:, lo:hi]
        tr = jnp.tanh(zg[:, :bw])
        ti = jnp.tanh(zg[:, bw:])
        kd = k2[d:d + 1]
        la2 = kd * tr + kd
        a = jnp.exp2(la2)
        th = jnp.tanh(la2 * (-LN2))
        root = th * lax.rsqrt(jnp.maximum(th * th + th, F32_TINY))
        return a, root * ((ti + 1.0) * uc)

    def run_sequence(p_ref, z_ref, length, h0f, h0b):
        lseg = length // SUBLANES
        steps = _tile(lseg, SCAN_STEPS, BF16_ROWS)
        n = steps * SUBLANES
        n_chunks = length // n

        def conv_body(k, carry):
            cu_ref[pl.ds(pl.multiple_of(k * n, n), n), :] = conv_chunk(p_ref, k, n, n_chunks)
            return carry

        lax.fori_loop(0, n_chunks, conv_body, 0)

        def chunk_coeffs(k):
            return (coeffs(cu_ref[pl.ds(k * n, n), :], 0)
                    + coeffs(cu_ref[pl.ds((n_chunks - 1 - k) * n, n), :], 1))

        def chunk_scan(k, co, carry):
            af, bf, ab, bb = co
            hf, pf, hb, pb = carry
            rf = k * n
            rb = (n_chunks - 1 - k) * n
            hs_f, ps_f, hs_b, ps_b = [None] * steps, [None] * steps, [None] * steps, [None] * steps
            for m in range(steps):
                sf = slice(m * SUBLANES, (m + 1) * SUBLANES)
                hf = af[sf] * hf + bf[sf]
                pf = af[sf] * pf
                hs_f[m], ps_f[m] = hf, pf
                mb = steps - 1 - m
                sb = slice(mb * SUBLANES, (mb + 1) * SUBLANES)
                hb = ab[sb] * hb + bb[sb]
                pb = ab[sb] * pb
                hs_b[mb], ps_b[mb] = hb, pb
            hl_ref[0, pl.ds(rf, n), :] = jnp.concatenate(hs_f, axis=0)
            pp_ref[0, pl.ds(rf, n), :] = jnp.concatenate(ps_f, axis=0)
            hl_ref[1, pl.ds(rb, n), :] = jnp.concatenate(hs_b, axis=0)
            pp_ref[1, pl.ds(rb, n), :] = jnp.concatenate(ps_b, axis=0)
            return hf, pf, hb, pb

        carry = (jnp.zeros((SUBLANES, bw), F32), jnp.ones((SUBLANES, bw), F32),
                 jnp.zeros((SUBLANES, bw), F32), jnp.ones((SUBLANES, bw), F32))
        nxt = chunk_coeffs(0)
        for k in range(n_chunks):
            cur = nxt
            if k + 1 < n_chunks:
                nxt = chunk_coeffs(k + 1)
            carry = chunk_scan(k, cur, carry)
        ef, qf, eb, qb = carry

        rows_f, rows_b = [None] * SUBLANES, [None] * SUBLANES
        c = h0f
        for s in range(SUBLANES):
            rows_f[s] = c
            c = qf[s:s + 1] * c + ef[s:s + 1]
        hf_end = c
        c = h0b
        for s in range(SUBLANES - 1, -1, -1):
            rows_b[s] = c
            c = qb[s:s + 1] * c + eb[s:s + 1]
        hb_end = c
        cf = jnp.concatenate(rows_f, axis=0)[None]
        cbk = jnp.concatenate(rows_b, axis=0)[None]

        def out_body(k, carry):
            r0 = pl.multiple_of(k * n, n)
            sl = pl.ds(r0, n)
            hf = hl_ref[0, sl, :].reshape(steps, SUBLANES, bw) + pp_ref[0, sl, :].reshape(steps, SUBLANES, bw) * cf
            hb = hl_ref[1, sl, :].reshape(steps, SUBLANES, bw) + pp_ref[1, sl, :].reshape(steps, SUBLANES, bw) * cbk
            g = unpack(p_ref, r0, n, True)
            gate = (g * (0.25 * SQRT2)) * (jnp.tanh(g * 0.5) + 1.0)
            z = (hf + hb).reshape(n, bw) * gate
            for s in range(nslab):
                zs_ref[s, 0:n, :] = z[:, s * LANES:(s + 1) * LANES]
            for seg in range(SUBLANES):
                piece = jnp.concatenate([zs_ref[s, pl.ds(seg, steps, stride=SUBLANES), :]
                                         for s in range(nslab)], axis=1)
                t0 = pl.multiple_of(seg * lseg + k * steps, steps)
                z_ref[pl.ds(t0, steps), :] = piece.astype(z_ref.dtype)
            return carry

        lax.fori_loop(0, n_chunks, out_body, 0)
        return hf_end, hb_end

    zero = jnp.zeros((1, bw), F32)
    hf0, hb0 = run_sequence(pc_ref, zc_ref, c_len, zero, zero)
    run_sequence(pl_ref, zl_ref, s_len, hf0, hb0)


def _rglru_call(p_lat, p_ctx, conv_w, conv_b, w_gate, b_gate, lam, side=()):
    b, nsl, s_len, _ = p_lat.shape
    c_len = p_ctx.shape[2]
    r = nsl * LANES
    nblk, bw, _ = w_gate.shape
    nslab = bw // LANES
    lmax = max(s_len, c_len)
    chunk = SUBLANES * SCAN_STEPS
    side_specs, side_shapes = _side_specs(side, b * nblk, lambda i, j: i * nblk + j)
    kern = functools.partial(_rglru_kernel, s_len=s_len, c_len=c_len, bw=bw, n_side=len(side))
    out = pl.pallas_call(
        kern,
        grid=(b, nblk),
        in_specs=[pl.BlockSpec((None, nslab, s_len, LANES), lambda i, j: (i, j, 0, 0)),
                  pl.BlockSpec((None, nslab, c_len, LANES), lambda i, j: (i, j, 0, 0)),
                  pl.BlockSpec((CONV_W, bw), lambda i, j: (0, j)),
                  pl.BlockSpec((1, bw), lambda i, j: (0, j)),
                  pl.BlockSpec((None, bw, 4 * bw), lambda i, j: (j, 0, 0)),
                  pl.BlockSpec((None, 1, 4 * bw), lambda i, j: (j, 0, 0)),
                  pl.BlockSpec((2, bw), lambda i, j: (0, j))] + side_specs,
        out_specs=[pl.BlockSpec((None, s_len, bw), lambda i, j: (i, 0, j)),
                   pl.BlockSpec((None, c_len, bw), lambda i, j: (i, 0, j))] + side_specs,
        out_shape=[jax.ShapeDtypeStruct((b, s_len, r), BF16),
                   jax.ShapeDtypeStruct((b, c_len, r), BF16)] + side_shapes,
        scratch_shapes=[pltpu.VMEM((lmax, bw), F32),
                        pltpu.VMEM((2, lmax, bw), F32),
                        pltpu.VMEM((2, lmax, bw), F32),
                        pltpu.VMEM((nslab, chunk, LANES), F32)],
        compiler_params=_cparams("arbitrary", "arbitrary"),
        name="rglru",
    )(p_lat, p_ctx, conv_w, conv_b.reshape(1, r), w_gate, b_gate, lam, *side)
    return out[0], out[1], out[2:]


def _headnorm_kernel(x_ref, w_ref, o_ref, *, n_heads):
    w = w_ref[...]
    for h in range(n_heads):
        sl = slice(h * HEAD_DIM, (h + 1) * HEAD_DIM)
        x = x_ref[:, sl].astype(F32)
        inv = lax.rsqrt(jnp.mean(x * x, axis=-1, keepdims=True) + NORM_EPS)
        o_ref[:, sl] = ((x * inv) * w).astype(o_ref.dtype)


def _qkprep_call(x, col0, width, w):
    b, l, _ = x.shape
    tr = _tile(l, 256, BF16_ROWS)
    assert col0 % width == 0
    off = col0 // width
    return pl.pallas_call(
        functools.partial(_headnorm_kernel, n_heads=width // HEAD_DIM),
        grid=(b, l // tr),
        in_specs=[pl.BlockSpec((None, tr, width), lambda i, j: (i, j, off)),
                  pl.BlockSpec((1, HEAD_DIM), lambda i, j: (0, 0))],
        out_specs=pl.BlockSpec((None, tr, width), lambda i, j: (i, j, 0)),
        out_shape=jax.ShapeDtypeStruct((b, l, width), BF16),
        compiler_params=_cparams("arbitrary", "arbitrary"),
        name="ctx_key_norm",
    )(x, w.reshape(1, HEAD_DIM))


def _dot_nt(a, b):
    return lax.dot_general(a, b, (((1,), (1,)), ((), ())), preferred_element_type=F32)


def _attn_kernel(sink_ref, bias_ref, q_ref, kp_ref, km_ref, kn_ref, vp_ref, vm_ref, vn_ref, kc_ref, vc_ref,
                 g_ref, o_ref, *, groups, qblocks, n_steps):
    kvh = pl.program_id(1)
    step = pl.program_id(2)
    blk = ATTN_BLOCK
    kw = jnp.concatenate([kp_ref[...], km_ref[...], kn_ref[...]], axis=0)
    vw = jnp.concatenate([vp_ref[...], vm_ref[...], vn_ref[...]], axis=0)
    kc = kc_ref[...]
    vt_c = jnp.concatenate([vc_ref[...].T, jnp.ones((BF16_ROWS, kc.shape[0]), BF16)], axis=0)
    vt_w = jnp.concatenate([vw.T, jnp.ones((BF16_ROWS, vw.shape[0]), BF16)], axis=0)
    sink2 =jnp.concatenate([jnp.full((1, blk), sink_ref[kvh * groups + g] * LOG2E, F32)
                             for g in range(groups)], axis=1)
    hpu = ATTN_HEADS_PER_UNIT if groups % ATTN_HEADS_PER_UNIT == 0 else groups
    units = [(i, h0) for i in range(qblocks) for h0 in range(0, groups, hpu)]

    def scores(u):
        i, h0 = units[u]
        q = q_ref[i * blk:(i + 1) * blk, h0 * HEAD_DIM:(h0 + hpu) * HEAD_DIM]
        qs = jnp.concatenate([q[:, g * HEAD_DIM:(g + 1) * HEAD_DIM] for g in range(hpu)], axis=0)
        first = (step == 0).astype(jnp.int32) if i == 0 else 0
        last = (step == n_steps - 1).astype(jnp.int32) if i == qblocks - 1 else 0
        bias = bias_ref[first + 2 * last, :, 0:hpu * blk]
        return _dot_nt(kc, qs), _dot_nt(kw[i * blk:(i + 3) * blk], qs) + bias

    ahead = [scores(u) for u in range(min(SCORE_LOOKAHEAD, len(units)))]
    for u, (i, h0) in enumerate(units):
        s_c, s_w = ahead.pop(0)
        if u + SCORE_LOOKAHEAD < len(units):
            ahead.append(scores(u + SCORE_LOOKAHEAD))
        sk = sink2[:, h0 * blk:(h0 + hpu) * blk]
        m = jnp.maximum(jnp.maximum(jnp.max(s_c, axis=0, keepdims=True),
                                    jnp.max(s_w, axis=0, keepdims=True)), sk)
        p_c = jnp.exp2(s_c - m).astype(BF16)
        p_w = jnp.exp2(s_w - m).astype(BF16)
        ot = (jnp.dot(vt_c, p_c, preferred_element_type=F32)
              + jnp.dot(vt_w[:, i * blk:(i + 3) * blk], p_w, preferred_element_type=F32))
        den = ot[HEAD_DIM:HEAD_DIM + 1, :] + jnp.exp2(sk - m)
        ot = ot[:HEAD_DIM, :] * (1.0 / den)
        o = jnp.concatenate([ot[:, g * blk:(g + 1) * blk].T for g in range(hpu)], axis=1)
        cols = slice(h0 * HEAD_DIM, (h0 + hpu) * HEAD_DIM)
        gate = _silu(g_ref[i * blk:(i + 1) * blk, cols].astype(F32))
        o_ref[i * blk:(i + 1) * blk, cols] = (o * gate).astype(o_ref.dtype)


def _band_bias(groups):
    blk = ATTN_BLOCK
    ki = jnp.arange(3 * blk)[:, None]
    qi = jnp.arange(blk)[None, :]
    band = jnp.abs(ki - blk - qi) <= WINDOW
    out = []
    for var in range(4):
        ok = band & ((ki >= blk) | (var % 2 == 0)) & ((ki < 2 * blk) | (var // 2 == 0))
        out.append(jnp.tile(jnp.where(ok, 0.0, NEG_INF).astype(F32), (1, groups)))
    return jnp.stack(out)


def _attn_call(sink, kcn, qkvg, kvc, attn_w, kv_w):
    b, s_len, _ = qkvg.shape
    c_len = kcn.shape[1]
    n_kv = kv_w // HEAD_DIM
    groups = attn_w // kv_w
    gw = groups * HEAD_DIM
    blk = ATTN_BLOCK
    nb = s_len // blk
    qblocks = next(q for q in (16, 8, 4, 2, 1) if nb % q == 0)
    n_steps = nb // qblocks
    k_off = attn_w // HEAD_DIM
    v_off = (attn_w + kv_w) // HEAD_DIM
    g_off = (attn_w + 2 * kv_w) // gw
    assert (attn_w + 2 * kv_w) % gw == 0

    def edge_spec(prev, off):
        if prev:
            return pl.BlockSpec((None, blk, HEAD_DIM),
                                lambda i, h, j: (i, jnp.maximum(j * qblocks - 1, 0), h + off))
        return pl.BlockSpec((None, blk, HEAD_DIM),
                            lambda i, h, j: (i, jnp.minimum((j + 1) * qblocks, nb - 1), h + off))

    def main_spec(off):
        return pl.BlockSpec((None, qblocks * blk, HEAD_DIM), lambda i, h, j: (i, j, h + off))

    def tile_spec(off):
        return pl.BlockSpec((None, qblocks * blk, gw), lambda i, h, j: (i, j, h + off))

    return pl.pallas_call(
        functools.partial(_attn_kernel, groups=groups, qblocks=qblocks, n_steps=n_steps),
        grid=(b, n_kv, n_steps),
        in_specs=[pl.BlockSpec(memory_space=pltpu.SMEM),
                  pl.BlockSpec((4, 3 * blk, groups * blk), lambda i, h, j: (0, 0, 0)),
                  tile_spec(0),
                  edge_spec(True, k_off), main_spec(k_off), edge_spec(False, k_off),
                  edge_spec(True, v_off), main_spec(v_off), edge_spec(False, v_off),
                  pl.BlockSpec((None, c_len, HEAD_DIM), lambda i, h, j: (i, 0, h)),
                  pl.BlockSpec((None, c_len, HEAD_DIM), lambda i, h, j: (i, 0, h + n_kv)),
                  tile_spec(g_off)],
        out_specs=tile_spec(0),
        out_shape=jax.ShapeDtypeStruct((b, s_len, attn_w), BF16),
        compiler_params=_cparams("arbitrary", "arbitrary", "arbitrary"),
        name="attention",
    )(sink, _band_bias(groups), qkvg, qkvg, qkvg, qkvg, qkvg, qkvg, qkvg, kcn, kvc, qkvg)


def _rope_tables(s_len):
    t = jnp.arange(s_len, dtype=jnp.int32)
    pos = jnp.stack([t // GRID_W, t % GRID_W], axis=1).astype(F32)
    half = HEAD_DIM // 2
    quarter = half // 2
    inv_freq = ROPE_BASE ** (-jnp.arange(quarter, dtype=F32) * (2.0 / half))
    ang = pos[:, :, None] * inv_freq[None, None, :]
    cos = jnp.concatenate([jnp.cos(ang), jnp.cos(ang)], axis=-1).reshape(s_len, HEAD_DIM)
    sin = jnp.concatenate([-jnp.sin(ang), jnp.sin(ang)], axis=-1).reshape(s_len, HEAD_DIM)
    return cos, sin


def kernel(x, c, ctx, c_ctx, w_mod, b_mod, norm_w, rg_w_in, rg_conv_w, rg_conv_b, rg_w_r, rg_b_r, rg_w_i,
           rg_b_i, rg_lam, rg_w_out, at_w_in, at_q_norm, at_k_norm, at_sink, at_w_out):
    b, s_len, d = x.shape
    c_len = ctx.shape[1]
    depth = w_mod.shape[0]
    assert depth == 2 and b < MOD_ROWS
    ctx_row = b

    c8 = jnp.concatenate([c, c_ctx[None, :], jnp.zeros((MOD_ROWS - b - 1, d), F32)], axis=0)
    mod = _mod_call(c8, w_mod, b_mod)
    shift, scale, gate = mod[:, :, :d], mod[:, :, d:2 * d], mod[:, :, 2 * d:]

    r = rg_w_out.shape[1]
    w_in = rg_w_in[0].astype(BF16)
    p_lat = _mm_ug_call(x, shift[0], scale[0], norm_w[0], None, w_in, r)
    p_ctx = _mm_ug_call(ctx, shift[0], scale[0], norm_w[0], ctx_row, w_in, r)
    w_gate = (0.5 * jnp.concatenate([rg_w_r[0, 0], rg_w_i[0, 0], rg_w_r[0, 1], rg_w_i[0, 1]], axis=-1)).astype(BF16)
    nblk, bw = rg_w_r.shape[2], rg_w_r.shape[3]
    b_gate = 0.5 * jnp.concatenate([rg_b_r[0, 0].reshape(nblk, 1, bw), rg_b_i[0, 0].reshape(nblk, 1, bw),
                                    rg_b_r[0, 1].reshape(nblk, 1, bw), rg_b_i[0, 1].reshape(nblk, 1, bw)], axis=-1)
    z, zc, (w_out, at_w_in_bf, at_w_out_bf) = _rglru_call(
        p_lat, p_ctx, rg_conv_w[0], rg_conv_b[0], w_gate, b_gate, rg_lam[0],
        (rg_w_out[0], at_w_in[0], at_w_out[0]))
    x = _mm_resid_call(z.reshape(b * s_len, r), w_out, x.reshape(b * s_len, d), gate[0], s_len, None)
    x = x.reshape(b, s_len, d)
    ctx = _mm_resid_call(zc.reshape(b * c_len, r), w_out, ctx.reshape(b * c_len, d), gate[0], c_len, ctx_row)
    ctx = ctx.reshape(b, c_len, d)

    attn_w = at_w_out.shape[1]
    kv_w = (at_w_in.shape[2] - 2 * attn_w) // 2
    hc = _adaln_call(ctx, shift[1], scale[1], norm_w[1], ctx_row)
    n_heads, n_kv = attn_w // HEAD_DIM, kv_w // HEAD_DIM
    w_in = at_w_in_bf
    k_norm = at_k_norm[0]
    nw = jnp.concatenate([jnp.tile(at_q_norm[0] * SCORE_SCALE, n_heads), jnp.tile(k_norm, n_kv)])
    nw = nw.reshape(1, attn_w + kv_w)
    cos, sin = _rope_tables(s_len)
    qkvg = _mm_qkvg_call(x, shift[1], scale[1], norm_w[1], w_in, cos, sin, nw).reshape(b, s_len, -1)
    kvc = _mm_call(hc.reshape(b * c_len, d), w_in, attn_w, 2 * kv_w, BF16).reshape(b, c_len, 2 * kv_w)
    kcn = _qkprep_call(kvc, 0, kv_w, k_norm)
    z = _attn_call(at_sink[0], kcn, qkvg, kvc, attn_w, kv_w)
    x = _mm_resid_call(z.reshape(b * s_len, attn_w), at_w_out_bf, x.reshape(b * s_len, d),
                       gate[1], s_len, None)
    return x.reshape(b, s_len, d)
```

```python
import functools

import jax
import jax.numpy as jnp
from jax import lax
from jax.experimental import pallas as pl
from jax.experimental.pallas import tpu as pltpu

F32 = jnp.float32
BF16 = jnp.bfloat16

HEAD_DIM = 128
WINDOW = 128
ATTN_BLOCK = 128
GRID_W = 64
ROPE_BASE = 10000.0
NORM_EPS = 1e-6
NEG_INF = -1e30
LRU_C = 8.0
CONV_W = 4
CONV_PAD_LO = 2
LOG2E = 1.4426950408889634
LN2 = 0.6931471805599453
SCORE_SCALE = HEAD_DIM ** -0.5 * LOG2E
SQRT2 = 1.4142135623730951
F32_TINY = 1.1754943508222875e-38

SUBLANES = 8
LANES = 128
BF16_ROWS = 16
MOD_ROWS = 8
VMEM_LIMIT = 56 * 1024 * 1024
SCAN_STEPS = 64
NORM_ROW_BLOCKS = 4
ADALN_SEGS = 2
ROW_TILE_GROUP = 2
SCORE_LOOKAHEAD = 3
ATTN_HEADS_PER_UNIT = 2


def _cparams(*sem):
    return pltpu.CompilerParams(dimension_semantics=sem, vmem_limit_bytes=VMEM_LIMIT)


def _tile(n, pref, unit):
    if n <= pref:
        return n
    t = (pref // unit) * unit
    while n % t:
        t -= unit
    return t


def _sigmoid(x):
    return 1.0 / (1.0 + jnp.exp(-x))


def _silu(x):
    return x * _sigmoid(x)


def _mod_kernel(c_ref, w_ref, b_ref, o_ref):
    s = _silu(c_ref[...]).astype(BF16)
    o_ref[...] = jnp.dot(s, w_ref[...].astype(BF16), preferred_element_type=F32) + b_ref[...]


def _mod_call(c8, w_mod, b_mod):
    depth, d, n3 = w_mod.shape
    tn = _tile(n3, 768, LANES)
    return pl.pallas_call(
        _mod_kernel,
        grid=(depth, n3 // tn),
        in_specs=[pl.BlockSpec((MOD_ROWS, d), lambda l, j: (0, 0)),
                  pl.BlockSpec((None, d, tn), lambda l, j: (l, 0, j)),
                  pl.BlockSpec((None, 1, tn), lambda l, j: (l, 0, j))],
        out_specs=pl.BlockSpec((None, MOD_ROWS, tn), lambda l, j: (l, 0, j)),
        out_shape=jax.ShapeDtypeStruct((depth, MOD_ROWS, n3), F32),
        compiler_params=_cparams("arbitrary", "arbitrary"),
        name="mod",
    )(c8, w_mod, b_mod.reshape(depth, 1, n3))


def _adaln_kernel(x_ref, sh_ref, sc_ref, nw_ref, o_ref, *, row):
    r = pl.program_id(0) if row is None else row
    x = x_ref[...]
    inv = lax.rsqrt(jnp.mean(x * x, axis=-1, keepdims=True) + NORM_EPS)
    xn = (x * inv) * nw_ref[...]
    o_ref[...] = (xn * (1.0 + sc_ref[pl.ds(r, 1), :]) + sh_ref[pl.ds(r, 1), :]).astype(o_ref.dtype)


def _adaln_call(x, shift, scale, nw, row):
    b, l, d = x.shape
    tr = _tile(l, 512, BF16_ROWS)
    return pl.pallas_call(
        functools.partial(_adaln_kernel, row=row),
        grid=(b, l // tr),
        in_specs=[pl.BlockSpec((None, tr, d), lambda i, j: (i, j, 0)),
                  pl.BlockSpec((MOD_ROWS, d), lambda i, j: (0, 0)),
                  pl.BlockSpec((MOD_ROWS, d), lambda i, j: (0, 0)),
                  pl.BlockSpec((1, d), lambda i, j: (0, 0))],
        out_specs=pl.BlockSpec((None, tr, d), lambda i, j: (i, j, 0)),
        out_shape=jax.ShapeDtypeStruct((b, l, d), BF16),
        compiler_params=_cparams("arbitrary", "arbitrary"),
        name="adaln",
    )(x, shift, scale, nw.reshape(1, d))


def _mm_kernel(a_ref, w_ref, o_ref):
    o_ref[...] = jnp.dot(a_ref[...], w_ref[...], preferred_element_type=F32).astype(o_ref.dtype)


def _mm_call(a, w, col0, ncols, out_dtype):
    m, k = a.shape
    tm = _tile(m, 1024, BF16_ROWS)
    tn = _tile(ncols, 512, LANES)
    assert col0 % tn == 0
    off = col0 // tn
    return pl.pallas_call(
        _mm_kernel,
        grid=(m // tm, ncols // tn),
        in_specs=[pl.BlockSpec((tm, k), lambda i, j: (i, 0)),
                  pl.BlockSpec((k, tn), lambda i, j: (0, j + off))],
        out_specs=pl.BlockSpec((tm, tn), lambda i, j: (i, j)),
        out_shape=jax.ShapeDtypeStruct((m, ncols), out_dtype),
        compiler_params=_cparams("arbitrary", "arbitrary"),
        name="matmul",
    )(a, w)


def _adaln_rows(x, sh_ref, sc_ref, nwk_ref, r):
    inv = lax.rsqrt(jnp.mean(x * x, axis=-1, keepdims=True) + NORM_EPS)
    xn = (x * inv) * nwk_ref[...]
    return (xn * (1.0 + sc_ref[pl.ds(r, 1), :]) + sh_ref[pl.ds(r, 1), :]).astype(BF16)


def _mm_qkvg_kernel(x_ref, sh_ref, sc_ref, nwk_ref, w_ref, cos_ref, sin_ref, nw_ref, o_ref, h_ref, *,
                    n_norm_tiles, tiles_per_seq):
    j = pl.program_id(1)
    slot = pl.program_id(2)
    tile = pl.program_id(0) * h_ref.shape[0] + slot
    tm = x_ref.shape[0]
    rb = tm // NORM_ROW_BLOCKS if tm % (NORM_ROW_BLOCKS * BF16_ROWS) == 0 else tm
    quarter = HEAD_DIM // 4

    def qk_rows(r0, a):
        y = jnp.dot(a, w_ref[...], preferred_element_type=F32)
        cos = cos_ref[r0:r0 + rb, :]
        sin = sin_ref[r0:r0 + rb, :]
        lane = lax.broadcasted_iota(jnp.int32, (rb, HEAD_DIM), 1)
        x1_lanes = (lane % (2 * quarter)) < quarter
        for h in range(y.shape[1] // HEAD_DIM):
            sl = slice(h * HEAD_DIM, (h + 1) * HEAD_DIM)
            x = y[:, sl]
            inv = lax.rsqrt(jnp.mean(x * x, axis=-1, keepdims=True) + NORM_EPS)
            xn = (x * inv) * nw_ref[:, sl]
            partner = jnp.where(x1_lanes, pltpu.roll(xn, HEAD_DIM - quarter, 1), pltpu.roll(xn, quarter, 1))
            o_ref[r0:r0 + rb, sl] = (xn * cos + partner * sin).astype(o_ref.dtype)

    @pl.when(j == 0)
    def _():
        r = tile // tiles_per_seq
        for r0 in range(0, tm, rb):
            a = _adaln_rows(x_ref[r0:r0 + rb, :], sh_ref, sc_ref, nwk_ref, r)
            h_ref[slot, r0:r0 + rb, :] = a
            qk_rows(r0, a)

    @pl.when((j > 0) & (j < n_norm_tiles))
    def _():
        for r0 in range(0, tm, rb):
            qk_rows(r0, h_ref[slot, r0:r0 + rb, :])

    @pl.when(j >= n_norm_tiles)
    def _():
        o_ref[...] = jnp.dot(h_ref[slot], w_ref[...], preferred_element_type=F32).astype(o_ref.dtype)


def _mm_qkvg_call(x, shift, scale, nwk, w, cos, sin, nw):
    b, s_len, k = x.shape
    m = b * s_len
    n = w.shape[1]
    tm = _tile(s_len, 512, BF16_ROWS)
    tn = _tile(nw.shape[1], 1024, HEAD_DIM)
    assert n % tn == 0 and nw.shape[1] >= tn
    n_norm_tiles = nw.shape[1] // tn
    t_per_seq = s_len // tm
    grp = next(g for g in (2 * ROW_TILE_GROUP, ROW_TILE_GROUP, 1) if (m // tm) % g == 0)

    def x_map(i, j, s):
        return (i * grp + jnp.where(j == 0, s, grp - 1), 0)

    def pos_map(i, j, s):
        return ((i * grp + s) % t_per_seq, 0)

    return pl.pallas_call(
        functools.partial(_mm_qkvg_kernel, n_norm_tiles=n_norm_tiles, tiles_per_seq=t_per_seq),
        grid=(m // (tm * grp), n // tn, grp),
        in_specs=[pl.BlockSpec((tm, k), x_map),
                  pl.BlockSpec((MOD_ROWS, k), lambda i, j, s: (0, 0)),
                  pl.BlockSpec((MOD_ROWS, k), lambda i, j, s: (0, 0)),
                  pl.BlockSpec((1, k), lambda i, j, s: (0, 0)),
                  pl.BlockSpec((k, tn), lambda i, j, s: (0, j)),
                  pl.BlockSpec((tm, HEAD_DIM), pos_map),
                  pl.BlockSpec((tm, HEAD_DIM), pos_map),
                  pl.BlockSpec((1, tn), lambda i, j, s: (0, jnp.minimum(j, n_norm_tiles - 1)))],
        out_specs=pl.BlockSpec((tm, tn), lambda i, j, s: (i * grp + s, j)),
        out_shape=jax.ShapeDtypeStruct((m, n), BF16),
        scratch_shapes=[pltpu.VMEM((grp, tm, k), BF16)],
        compiler_params=_cparams("arbitrary", "arbitrary", "arbitrary"),
        name="matmul_qkvg",
    )(x.reshape(m, k), shift, scale, nwk.reshape(1, k), w, cos, sin, nw)


def _mm_ug_kernel(x_ref, sh_ref, sc_ref, nwk_ref, wu_ref, wg_ref, o_ref, h_ref, *, tmm, row, n_m):
    nbat, _, _, k = x_ref.shape
    j = pl.program_id(1)
    slot = pl.program_id(2)
    tile = pl.program_id(0) * h_ref.shape[0] + slot

    def project(bi, seg0, nseg, a):
        u = jnp.dot(a, wu_ref[...], preferred_element_type=F32).astype(BF16).astype(F32)
        g = jnp.dot(a, wg_ref[...], preferred_element_type=F32).astype(BF16).astype(F32)
        word = lax.bitcast_convert_type(u, jnp.uint32) | (lax.bitcast_convert_type(g, jnp.uint32) >> 16)
        for ds in range(nseg):
            for s in range(o_ref.shape[1]):
                o_ref[bi, s, pl.ds(seg0 + ds, tmm, stride=SUBLANES), :] = (
                    word[ds * tmm:(ds + 1) * tmm, s * LANES:(s + 1) * LANES])

    @pl.when(j == 0)
    def _():
        for bi in range(nbat):
            r = ((tile // n_m) * nbat + bi) if row is None else row
            for seg0 in range(0, SUBLANES, ADALN_SEGS):
                x = x_ref[bi, seg0:seg0 + ADALN_SEGS].reshape(ADALN_SEGS * tmm, k)
                a = _adaln_rows(x, sh_ref, sc_ref, nwk_ref, r)
                h_ref[slot, bi, seg0:seg0 + ADALN_SEGS] = a.reshape(ADALN_SEGS, tmm, k)
                project(bi, seg0, ADALN_SEGS, a)

    @pl.when(j > 0)
    def _():
        for bi in range(nbat):
            project(bi, 0, SUBLANES, h_ref[slot, bi].reshape(SUBLANES * tmm, k))


def _side_specs(side, steps, step_of):
    specs, shapes = [], []
    for t in side:
        rows = BF16_ROWS
        while t.shape[0] % rows or t.shape[0] // rows > steps:
            rows += BF16_ROWS
        last = t.shape[0] // rows - 1
        specs.append(pl.BlockSpec((rows, t.shape[1]),
                                  lambda *ids, last=last: (jnp.minimum(step_of(*ids), last), 0)))
        shapes.append(jax.ShapeDtypeStruct(t.shape, BF16))
    return specs, shapes


def _mm_ug_call(x, shift, scale, nwk, row, w, r):
    b, l, k = x.shape
    lseg = l // SUBLANES
    tmm = _tile(lseg, 64, BF16_ROWS)
    nbat = _tile(b, max(1, 64 // tmm), 1)
    tn = _tile(r, 512, LANES)
    nslab = tn // LANES
    n_m = lseg // tmm
    n_tiles = (b // nbat) * n_m
    grp = ROW_TILE_GROUP if n_tiles % ROW_TILE_GROUP == 0 else 1
    grid = (n_tiles // grp, r // tn, grp)

    def x_map(i, j, s):
        t = i * grp + jnp.where(j == 0, s, grp - 1)
        return (t // n_m, 0, t % n_m, 0)

    def out_map(i, j, s):
        t = i * grp + s
        return (t // n_m, j, t % n_m, 0)

    return pl.pallas_call(
        functools.partial(_mm_ug_kernel, tmm=tmm, row=row, n_m=n_m),
        grid=grid,
        in_specs=[pl.BlockSpec((nbat, SUBLANES, tmm, k), x_map),
                  pl.BlockSpec((MOD_ROWS, k), lambda i, j, s: (0, 0)),
                  pl.BlockSpec((MOD_ROWS, k), lambda i, j, s: (0, 0)),
                  pl.BlockSpec((1, k), lambda i, j, s: (0, 0)),
                  pl.BlockSpec((k, tn), lambda i, j, s: (0, j)),
                  pl.BlockSpec((k, tn), lambda i, j, s: (0, j + r // tn))],
        out_specs=pl.BlockSpec((nbat, nslab, SUBLANES * tmm, LANES), out_map),
        out_shape=jax.ShapeDtypeStruct((b, r // LANES, l, LANES), jnp.uint32),
        scratch_shapes=[pltpu.VMEM((grp, nbat, SUBLANES, tmm, k), BF16)],
        compiler_params=_cparams("arbitrary", "arbitrary", "arbitrary"),
        name="matmul_ug",
    )(x.reshape(b, SUBLANES, lseg, k), shift, scale, nwk.reshape(1, k), w, w)


def _mm_resid_kernel(a_ref, w_ref, x_ref, g_ref, o_ref, *, row, tiles_per_batch):
    r = (pl.program_id(0) // tiles_per_batch) if row is None else row
    y = jnp.dot(a_ref[...], w_ref[...], preferred_element_type=F32)
    o_ref[...] = x_ref[...] + g_ref[pl.ds(r, 1), :] * y


def _mm_resid_call(a, w, x, gate, rows_per_batch, row):
    m, k = a.shape
    n = w.shape[1]
    tm = _tile(rows_per_batch, 1024, BF16_ROWS)
    tn = _tile(n, 512, LANES)
    return pl.pallas_call(
        functools.partial(_mm_resid_kernel, row=row, tiles_per_batch=rows_per_batch // tm),
        grid=(m // tm, n // tn),
        in_specs=[pl.BlockSpec((tm, k), lambda i, j: (i, 0)),
                  pl.BlockSpec((k, tn), lambda i, j: (0, j)),
                  pl.BlockSpec((tm, tn), lambda i, j: (i, j)),
                  pl.BlockSpec((MOD_ROWS, tn), lambda i, j: (0, j))],
        out_specs=pl.BlockSpec((tm, tn), lambda i, j: (i, j)),
        out_shape=jax.ShapeDtypeStruct((m, n), F32),
        compiler_params=_cparams("arbitrary", "arbitrary"),
        name="matmul_resid",
    )(a, w, x, gate)


def _rglru_kernel(pl_ref, pc_ref, cw_ref, cb_ref, wg_ref, bg_ref, lam_ref, *rest, s_len, c_len, bw, n_side):
    side_in, (zl_ref, zc_ref) = rest[:n_side], rest[n_side:n_side + 2]
    side_out = rest[n_side + 2:2 * n_side + 2]
    cu_ref, hl_ref, pp_ref, zs_ref = rest[2 * n_side + 2:]
    for src, dst in zip(side_in, side_out):
        dst[...] = src[...].astype(dst.dtype)
    nslab = bw // LANES
    cw = cw_ref[...]
    cb = cb_ref[...]
    nlam = -lam_ref[...]
    softplus = jnp.maximum(nlam, 0.0) + jnp.log1p(jnp.exp(-jnp.abs(nlam)))
    k2 = (-0.5 * LRU_C * LOG2E) * softplus
    sub = lax.broadcasted_iota(jnp.int32, (SUBLANES, bw), 0)

    def unpack(p_ref, r0, n, low):
        parts = []
        for s in range(nslab):
            w = p_ref[s, pl.ds(r0, n), :]
            w = (w << 16) if low else (w & jnp.uint32(0xFFFF0000))
            parts.append(lax.bitcast_convert_type(w, F32))
        return jnp.concatenate(parts, axis=1)

    def seg_down(v):
        return jnp.where(sub == 0, 0.0, pltpu.roll(v, 1, 0))

    def seg_up(v):
        return jnp.where(sub == SUBLANES - 1, 0.0, pltpu.roll(v, SUBLANES - 1, 0))

    def conv_chunk(p_ref, k, n, n_chunks):
        length = n * n_chunks
        r0 = pl.multiple_of(k * n, n)
        lo = CONV_PAD_LO * SUBLANES
        hi = (CONV_W - 1 - CONV_PAD_LO) * SUBLANES
        cur = unpack(p_ref, r0, n, False)
        p0 = pl.multiple_of(jnp.where(k == 0, length - lo, r0 - lo), SUBLANES)
        n0 = pl.multiple_of(jnp.where(k == n_chunks - 1, 0, r0 + n), SUBLANES)
        prev = unpack(p_ref, p0, lo, False)
        nxt = unpack(p_ref, n0, hi, False)
        prev_edge = jnp.concatenate([seg_down(prev[i * SUBLANES:(i + 1) * SUBLANES])
                                     for i in range(CONV_PAD_LO)], axis=0)
        nxt_edge = jnp.concatenate([seg_up(nxt[i * SUBLANES:(i + 1) * SUBLANES])
                                    for i in range(CONV_W - 1 - CONV_PAD_LO)], axis=0)
        prev = jnp.where(k == 0, prev_edge, prev)
        nxt = jnp.where(k == n_chunks - 1, nxt_edge, nxt)
        ext = jnp.concatenate([prev, cur, nxt], axis=0)
        acc = cb + cw[0:1] * ext[0:n]
        for t in range(1, CONV_W):
            acc = acc + cw[t:t + 1] * ext[t * SUBLANES:t * SUBLANES + n]
        return acc

    def coeffs(uc, d):
        lo, hi = 2 * d * bw, 2 * (d + 1) * bw
        zg = jnp.dot(uc.astype(BF16), wg_ref[:, lo:hi], preferred_element_type=F32) + bg_ref[:, lo:hi]
        tr = jnp.tanh(zg[:, :bw])
        ti = jnp.tanh(zg[:, bw:])
        kd = k2[d:d + 1]
        la2 = kd * tr + kd
        a = jnp.exp2(la2)
        th = jnp.tanh(la2 * (-LN2))
        root = th * lax.rsqrt(jnp.maximum(th * th + th, F32_TINY))
        return a, root * ((ti + 1.0) * uc)

    def run_sequence(p_ref, z_ref, length, h0f, h0b):
        lseg = length // SUBLANES
        steps = _tile(lseg, SCAN_STEPS, BF16_ROWS)
        n = steps * SUBLANES
        n_chunks = length // n

        def conv_body(k, carry):
            cu_ref[pl.ds(pl.multiple_of(k * n, n), n), :] = conv_chunk(p_ref, k, n, n_chunks)
            return carry

        lax.fori_loop(0, n_chunks, conv_body, 0)

        def chunk_coeffs(k):
            return (coeffs(cu_ref[pl.ds(k * n, n), :], 0)
                    + coeffs(cu_ref[pl.ds((n_chunks - 1 - k) * n, n), :], 1))

        def chunk_scan(k, co, carry):
            af, bf, ab, bb = co
            hf, pf, hb, pb = carry
            rf = k * n
            rb = (n_chunks - 1 - k) * n
            hs_f, ps_f, hs_b, ps_b = [None] * steps, [None] * steps, [None] * steps, [None] * steps
            for m in range(steps):
                sf = slice(m * SUBLANES, (m + 1) * SUBLANES)
                hf = af[sf] * hf + bf[sf]
                pf = af[sf] * pf
                hs_f[m], ps_f[m] = hf, pf
                mb = steps - 1 - m
                sb = slice(mb * SUBLANES, (mb + 1) * SUBLANES)
                hb = ab[sb] * hb + bb[sb]
                pb = ab[sb] * pb
                hs_b[mb], ps_b[mb] = hb, pb
            hl_ref[0, pl.ds(rf, n), :] = jnp.concatenate(hs_f, axis=0)
            pp_ref[0, pl.ds(rf, n), :] = jnp.concatenate(ps_f, axis=0)
            hl_ref[1, pl.ds(rb, n), :] = jnp.concatenate(hs_b, axis=0)
            pp_ref[1, pl.ds(rb, n), :] = jnp.concatenate(ps_b, axis=0)
            return hf, pf, hb, pb

        carry = (jnp.zeros((SUBLANES, bw), F32), jnp.ones((SUBLANES, bw), F32),
                 jnp.zeros((SUBLANES, bw), F32), jnp.ones((SUBLANES, bw), F32))
        nxt = chunk_coeffs(0)
        for k in range(n_chunks):
            cur = nxt
            if k + 1 < n_chunks:
                nxt = chunk_coeffs(k + 1)
            carry = chunk_scan(k, cur, carry)
        ef, qf, eb, qb = carry

        rows_f, rows_b = [None] * SUBLANES, [None] * SUBLANES
        c = h0f
        for s in range(SUBLANES):
            rows_f[s] = c
            c = qf[s:s + 1] * c + ef[s:s + 1]
        hf_end = c
        c = h0b
        for s in range(SUBLANES - 1, -1, -1):
            rows_b[s] = c
            c = qb[s:s + 1] * c + eb[s:s + 1]
        hb_end = c
        cf = jnp.concatenate(rows_f, axis=0)[None]
        cbk = jnp.concatenate(rows_b, axis=0)[None]

        def out_body(k, carry):
            r0 = pl.multiple_of(k * n, n)
            sl = pl.ds(r0, n)
            hf = hl_ref[0, sl, :].reshape(steps, SUBLANES, bw) + pp_ref[0, sl, :].reshape(steps, SUBLANES, bw) * cf
            hb = hl_ref[1, sl, :].reshape(steps, SUBLANES, bw) + pp_ref[1, sl, :].reshape(steps, SUBLANES, bw) * cbk
            g = unpack(p_ref, r0, n, True)
            gate = (g * (0.25 * SQRT2)) * (jnp.tanh(g * 0.5) + 1.0)
            z = (hf + hb).reshape(n, bw) * gate
            for s in range(nslab):
                zs_ref[s, 0:n, :] = z[:, s * LANES:(s + 1) * LANES]
            for seg in range(SUBLANES):
                piece = jnp.concatenate([zs_ref[s, pl.ds(seg, steps, stride=SUBLANES), :]
                                         for s in range(nslab)], axis=1)
                t0 = pl.multiple_of(seg * lseg + k * steps, steps)
                z_ref[pl.ds(t0, steps), :] = piece.astype(z_ref.dtype)
            return carry

        lax.fori_loop(0, n_chunks, out_body, 0)
        return hf_end, hb_end

    zero = jnp.zeros((1, bw), F32)
    hf0, hb0 = run_sequence(pc_ref, zc_ref, c_len, zero, zero)
    run_sequence(pl_ref, zl_ref, s_len, hf0, hb0)


def _rglru_call(p_lat, p_ctx, conv_w, conv_b, w_gate, b_gate, lam, side=()):
    b, nsl, s_len, _ = p_lat.shape
    c_len = p_ctx.shape[2]
    r = nsl * LANES
    nblk, bw, _ = w_gate.shape
    nslab = bw // LANES
    lmax = max(s_len, c_len)
    chunk = SUBLANES * SCAN_STEPS
    side_specs, side_shapes = _side_specs(side, b * nblk, lambda i, j: i * nblk + j)
    kern = functools.partial(_rglru_kernel, s_len=s_len, c_len=c_len, bw=bw, n_side=len(side))
    out = pl.pallas_call(
        kern,
        grid=(b, nblk),
        in_specs=[pl.BlockSpec((None, nslab, s_len, LANES), lambda i, j: (i, j, 0, 0)),
                  pl.BlockSpec((None, nslab, c_len, LANES), lambda i, j: (i, j, 0, 0)),
                  pl.BlockSpec((CONV_W, bw), lambda i, j: (0, j)),
                  pl.BlockSpec((1, bw), lambda i, j: (0, j)),
                  pl.BlockSpec((None, bw, 4 * bw), lambda i, j: (j, 0, 0)),
                  pl.BlockSpec((None, 1, 4 * bw), lambda i, j: (j, 0, 0)),
                  pl.BlockSpec((2, bw), lambda i, j: (0, j))] + side_specs,
        out_specs=[pl.BlockSpec((None, s_len, bw), lambda i, j: (i, 0, j)),
                   pl.BlockSpec((None, c_len, bw), lambda i, j: (i, 0, j))] + side_specs,
        out_shape=[jax.ShapeDtypeStruct((b, s_len, r), BF16),
                   jax.ShapeDtypeStruct((b, c_len, r), BF16)] + side_shapes,
        scratch_shapes=[pltpu.VMEM((lmax, bw), F32),
                        pltpu.VMEM((2, lmax, bw), F32),
                        pltpu.VMEM((2, lmax, bw), F32),
                        pltpu.VMEM((nslab, chunk, LANES), F32)],
        compiler_params=_cparams("arbitrary", "arbitrary"),
        name="rglru",
    )(p_lat, p_ctx, conv_w, conv_b.reshape(1, r), w_gate, b_gate, lam, *side)
    return out[0], out[1], out[2:]


def _headnorm_kernel(x_ref, w_ref, o_ref, *, n_heads):
    w = w_ref[...]
    for h in range(n_heads):
        sl = slice(h * HEAD_DIM, (h + 1) * HEAD_DIM)
        x = x_ref[:, sl].astype(F32)
        inv = lax.rsqrt(jnp.mean(x * x, axis=-1, keepdims=True) + NORM_EPS)
        o_ref[:, sl] = ((x * inv) * w).astype(o_ref.dtype)


def _qkprep_call(x, col0, width, w):
    b, l, _ = x.shape
    tr = _tile(l, 256, BF16_ROWS)
    assert col0 % width == 0
    off = col0 // width
    return pl.pallas_call(
        functools.partial(_headnorm_kernel, n_heads=width // HEAD_DIM),
        grid=(b, l // tr),
        in_specs=[pl.BlockSpec((None, tr, width), lambda i, j: (i, j, off)),
                  pl.BlockSpec((1, HEAD_DIM), lambda i, j: (0, 0))],
        out_specs=pl.BlockSpec((None, tr, width), lambda i, j: (i, j, 0)),
        out_shape=jax.ShapeDtypeStruct((b, l, width), BF16),
        compiler_params=_cparams("arbitrary", "arbitrary"),
        name="ctx_key_norm",
    )(x, w.reshape(1, HEAD_DIM))


def _dot_nt(a, b):
    return lax.dot_general(a, b, (((1,), (1,)), ((), ())), preferred_element_type=F32)


def _attn_kernel(sink_ref, bias_ref, q_ref, kp_ref, km_ref, kn_ref, vp_ref, vm_ref, vn_ref, kc_ref, vc_ref,
                 g_ref, o_ref, *, groups, qblocks, n_steps):
    kvh = pl.program_id(1)
    step = pl.program_id(2)
    blk = ATTN_BLOCK
    kw = jnp.concatenate([kp_ref[...], km_ref[...], kn_ref[...]], axis=0)
    vw = jnp.concatenate([vp_ref[...], vm_ref[...], vn_ref[...]], axis=0)
    kc = kc_ref[...]
    vt_c = jnp.concatenate([vc_ref[...].T, jnp.ones((BF16_ROWS, kc.shape[0]), BF16)], axis=0)
    vt_w = jnp.concatenate([vw.T, jnp.ones((BF16_ROWS, vw.shape[0]), BF16)], axis=0)
    sink2 =jnp.concatenate([jnp.full((1, blk), sink_ref[kvh * groups + g] * LOG2E, F32)
                             for g in range(groups)], axis=1)
    hpu = ATTN_HEADS_PER_UNIT if groups % ATTN_HEADS_PER_UNIT == 0 else groups
    units = [(i, h0) for i in range(qblocks) for h0 in range(0, groups, hpu)]

    def scores(u):
        i, h0 = units[u]
        q = q_ref[i * blk:(i + 1) * blk, h0 * HEAD_DIM:(h0 + hpu) * HEAD_DIM]
        qs = jnp.concatenate([q[:, g * HEAD_DIM:(g + 1) * HEAD_DIM] for g in range(hpu)], axis=0)
        first = (step == 0).astype(jnp.int32) if i == 0 else 0
        last = (step == n_steps - 1).astype(jnp.int32) if i == qblocks - 1 else 0
        bias = bias_ref[first + 2 * last, :, 0:hpu * blk]
        return _dot_nt(kc, qs), _dot_nt(kw[i * blk:(i + 3) * blk], qs) + bias

    ahead = [scores(u) for u in range(min(SCORE_LOOKAHEAD, len(units)))]
    for u, (i, h0) in enumerate(units):
        s_c, s_w = ahead.pop(0)
        if u + SCORE_LOOKAHEAD < len(units):
            ahead.append(scores(u + SCORE_LOOKAHEAD))
        sk = sink2[:, h0 * blk:(h0 + hpu) * blk]
        m = jnp.maximum(jnp.maximum(jnp.max(s_c, axis=0, keepdims=True),
                                    jnp.max(s_w, axis=0, keepdims=True)), sk)
        p_c = jnp.exp2(s_c - m).astype(BF16)
        p_w = jnp.exp2(s_w - m).astype(BF16)
        ot = (jnp.dot(vt_c, p_c, preferred_element_type=F32)
              + jnp.dot(vt_w[:, i * blk:(i + 3) * blk], p_w, preferred_element_type=F32))
        den = ot[HEAD_DIM:HEAD_DIM + 1, :] + jnp.exp2(sk - m)
        ot = ot[:HEAD_DIM, :] * (1.0 / den)
        o = jnp.concatenate([ot[:, g * blk:(g + 1) * blk].T for g in range(hpu)], axis=1)
        cols = slice(h0 * HEAD_DIM, (h0 + hpu) * HEAD_DIM)
        gate = _silu(g_ref[i * blk:(i + 1) * blk, cols].astype(F32))
        o_ref[i * blk:(i + 1) * blk, cols] = (o * gate).astype(o_ref.dtype)


def _band_bias(groups):
    blk = ATTN_BLOCK
    ki = jnp.arange(3 * blk)[:, None]
    qi = jnp.arange(blk)[None, :]
    band = jnp.abs(ki - blk - qi) <= WINDOW
    out = []
    for var in range(4):
        ok = band & ((ki >= blk) | (var % 2 == 0)) & ((ki < 2 * blk) | (var // 2 == 0))
        out.append(jnp.tile(jnp.where(ok, 0.0, NEG_INF).astype(F32), (1, groups)))
    return jnp.stack(out)


def _attn_call(sink, kcn, qkvg, kvc, attn_w, kv_w):
    b, s_len, _ = qkvg.shape
    c_len = kcn.shape[1]
    n_kv = kv_w // HEAD_DIM
    groups = attn_w // kv_w
    gw = groups * HEAD_DIM
    blk = ATTN_BLOCK
    nb = s_len // blk
    qblocks = next(q for q in (16, 8, 4, 2, 1) if nb % q == 0)
    n_steps = nb // qblocks
    k_off = attn_w // HEAD_DIM
    v_off = (attn_w + kv_w) // HEAD_DIM
    g_off = (attn_w + 2 * kv_w) // gw
    assert (attn_w + 2 * kv_w) % gw == 0

    def edge_spec(prev, off):
        if prev:
            return pl.BlockSpec((None, blk, HEAD_DIM),
                                lambda i, h, j: (i, jnp.maximum(j * qblocks - 1, 0), h + off))
        return pl.BlockSpec((None, blk, HEAD_DIM),
                            lambda i, h, j: (i, jnp.minimum((j + 1) * qblocks, nb - 1), h + off))

    def main_spec(off):
        return pl.BlockSpec((None, qblocks * blk, HEAD_DIM), lambda i, h, j: (i, j, h + off))

    def tile_spec(off):
        return pl.BlockSpec((None, qblocks * blk, gw), lambda i, h, j: (i, j, h + off))

    return pl.pallas_call(
        functools.partial(_attn_kernel, groups=groups, qblocks=qblocks, n_steps=n_steps),
        grid=(b, n_kv, n_steps),
        in_specs=[pl.BlockSpec(memory_space=pltpu.SMEM),
                  pl.BlockSpec((4, 3 * blk, groups * blk), lambda i, h, j: (0, 0, 0)),
                  tile_spec(0),
                  edge_spec(True, k_off), main_spec(k_off), edge_spec(False, k_off),
                  edge_spec(True, v_off), main_spec(v_off), edge_spec(False, v_off),
                  pl.BlockSpec((None, c_len, HEAD_DIM), lambda i, h, j: (i, 0, h)),
                  pl.BlockSpec((None, c_len, HEAD_DIM), lambda i, h, j: (i, 0, h + n_kv)),
                  tile_spec(g_off)],
        out_specs=tile_spec(0),
        out_shape=jax.ShapeDtypeStruct((b, s_len, attn_w), BF16),
        compiler_params=_cparams("arbitrary", "arbitrary", "arbitrary"),
        name="attention",
    )(sink, _band_bias(groups), qkvg, qkvg, qkvg, qkvg, qkvg, qkvg, qkvg, kcn, kvc, qkvg)


def _rope_tables(s_len):
    t = jnp.arange(s_len, dtype=jnp.int32)
    pos = jnp.stack([t // GRID_W, t % GRID_W], axis=1).astype(F32)
    half = HEAD_DIM // 2
    quarter = half // 2
    inv_freq = ROPE_BASE ** (-jnp.arange(quarter, dtype=F32) * (2.0 / half))
    ang = pos[:, :, None] * inv_freq[None, None, :]
    cos = jnp.concatenate([jnp.cos(ang), jnp.cos(ang)], axis=-1).reshape(s_len, HEAD_DIM)
    sin = jnp.concatenate([-jnp.sin(ang), jnp.sin(ang)], axis=-1).reshape(s_len, HEAD_DIM)
    return cos, sin


def kernel(x, c, ctx, c_ctx, w_mod, b_mod, norm_w, rg_w_in, rg_conv_w, rg_conv_b, rg_w_r, rg_b_r, rg_w_i,
           rg_b_i, rg_lam, rg_w_out, at_w_in, at_q_norm, at_k_norm, at_sink, at_w_out):
    b, s_len, d = x.shape
    c_len = ctx.shape[1]
    depth = w_mod.shape[0]
    assert depth == 2 and b < MOD_ROWS
    ctx_row = b

    c8 = jnp.concatenate([c, c_ctx[None, :], jnp.zeros((MOD_ROWS - b - 1, d), F32)], axis=0)
    mod = _mod_call(c8, w_mod, b_mod)
    shift, scale, gate = mod[:, :, :d], mod[:, :, d:2 * d], mod[:, :, 2 * d:]

    r = rg_w_out.shape[1]
    w_in = rg_w_in[0].astype(BF16)
    p_lat = _mm_ug_call(x, shift[0], scale[0], norm_w[0], None, w_in, r)
    p_ctx = _mm_ug_call(ctx, shift[0], scale[0], norm_w[0], ctx_row, w_in, r)
    w_gate = (0.5 * jnp.concatenate([rg_w_r[0, 0], rg_w_i[0, 0], rg_w_r[0, 1], rg_w_i[0, 1]], axis=-1)).astype(BF16)
    nblk, bw = rg_w_r.shape[2], rg_w_r.shape[3]
    b_gate = 0.5 * jnp.concatenate([rg_b_r[0, 0].reshape(nblk, 1, bw), rg_b_i[0, 0].reshape(nblk, 1, bw),
                                    rg_b_r[0, 1].reshape(nblk, 1, bw), rg_b_i[0, 1].reshape(nblk, 1, bw)], axis=-1)
    z, zc, (w_out, at_w_in_bf, at_w_out_bf) = _rglru_call(
        p_lat, p_ctx, rg_conv_w[0], rg_conv_b[0], w_gate, b_gate, rg_lam[0],
        (rg_w_out[0], at_w_in[0], at_w_out[0]))
    x = _mm_resid_call(z.reshape(b * s_len, r), w_out, x.reshape(b * s_len, d), gate[0], s_len, None)
    x = x.reshape(b, s_len, d)
    ctx = _mm_resid_call(zc.reshape(b * c_len, r), w_out, ctx.reshape(b * c_len, d), gate[0], c_len, ctx_row)
    ctx = ctx.reshape(b, c_len, d)

    attn_w = at_w_out.shape[1]
    kv_w = (at_w_in.shape[2] - 2 * attn_w) // 2
    hc = _adaln_call(ctx, shift[1], scale[1], norm_w[1], ctx_row)
    n_heads, n_kv = attn_w // HEAD_DIM, kv_w // HEAD_DIM
    w_in = at_w_in_bf
    k_norm = at_k_norm[0]
    nw = jnp.concatenate([jnp.tile(at_q_norm[0] * SCORE_SCALE, n_heads), jnp.tile(k_norm, n_kv)])
    nw = nw.reshape(1, attn_w + kv_w)
    cos, sin = _rope_tables(s_len)
    qkvg = _mm_qkvg_call(x, shift[1], scale[1], norm_w[1], w_in, cos, sin, nw).reshape(b, s_len, -1)
    kvc = _mm_call(hc.reshape(b * c_len, d), w_in, attn_w, 2 * kv_w, BF16).reshape(b, c_len, 2 * kv_w)
    kcn = _qkprep_call(kvc, 0, kv_w, k_norm)
    z = _attn_call(at_sink[0], kcn, qkvg, kvc, attn_w, kv_w)
    x = _mm_resid_call(z.reshape(b * s_len, attn_w), at_w_out_bf, x.reshape(b * s_len, d),
                       gate[1], s_len, None)
    return x.reshape(b, s_len, d)
```
